```python
import math
import jax, jax.numpy as jnp
from jax import lax
import numpy as np

D_MODEL = 4096
BATCH = 1
SEQ = 8192
DEPTH = 4

N_MIXERS = 2
HEAD_DIM = 128
GQA_HEADS = D_MODEL // HEAD_DIM
GQA_KV_HEADS = GQA_HEADS // 4
DIFF_HEAD_DIM = 128
DIFF_HEADS = D_MODEL // (2 * DIFF_HEAD_DIM)
DIFF_V_DIM = 2 * DIFF_HEAD_DIM
N_EXPERTS = 16
EXPERT_FF = 896
EC_CAPACITY_FACTOR = 2
NUM_BUCKETS = 32
MAX_DISTANCE = 128
GRID_W = 64
BLOCK_Q = 128
ROPE_THETA = 10000.0
NORM_EPS = 1e-6

kernel_name = 'hybrid_axialgqa_diffattn_ecmoe_encoder'


def rms_norm(x, g):
    xf = x.astype(jnp.float32)
    y = xf * lax.rsqrt(jnp.mean(xf * xf, axis=-1, keepdims=True) + NORM_EPS)
    return (y * g.astype(jnp.float32)).astype(x.dtype)


def axial_rope(t, row, col):
    half = t.shape[-1] // 2
    freq = ROPE_THETA ** (-jnp.arange(0, half, 2, dtype=jnp.float32) / half)

    def rot(u, pos):
        ang = pos.astype(jnp.float32)[:, None] * freq[None, :]
        c = jnp.cos(ang)[None, :, None, :].astype(u.dtype)
        s = jnp.sin(ang)[None, :, None, :].astype(u.dtype)
        u1, u2 = jnp.split(u, 2, axis=-1)
        return jnp.concatenate([u1 * c - u2 * s, u2 * c + u1 * s], axis=-1)

    return jnp.concatenate([rot(t[..., :half], row), rot(t[..., half:], col)], axis=-1)


def t5_bucket(rel):
    nb = NUM_BUCKETS // 2
    max_exact = nb // 2
    n = jnp.abs(rel)
    large = max_exact + (jnp.log(jnp.maximum(n, 1).astype(jnp.float32) / max_exact)
                         / math.log(MAX_DISTANCE / max_exact) * (nb - max_exact)).astype(jnp.int32)
    large = jnp.minimum(large, nb - 1)
    return jnp.where(rel > 0, nb, 0) + jnp.where(n < max_exact, n, large)


def gqa_axial_mixer(h, w_in, q_gain, k_gain, w_out, row, col):
    B, S, _ = h.shape
    qd = GQA_HEADS * HEAD_DIM
    kd = GQA_KV_HEADS * HEAD_DIM
    proj = h @ w_in
    q = proj[..., :qd].reshape(B, S, GQA_HEADS, HEAD_DIM)
    k = proj[..., qd:qd + kd].reshape(B, S, GQA_KV_HEADS, HEAD_DIM)
    v = proj[..., qd + kd:].reshape(B, S, GQA_KV_HEADS, HEAD_DIM)
    q = axial_rope(rms_norm(q, q_gain), row, col)
    k = axial_rope(rms_norm(k, k_gain), row, col)
    G = GQA_HEADS // GQA_KV_HEADS
    nblk = S // BLOCK_Q
    qb = q.reshape(B, nblk, BLOCK_Q, GQA_KV_HEADS, G, HEAD_DIM).transpose(1, 0, 2, 3, 4, 5)
    scale = HEAD_DIM ** -0.5

    def block(qblk):
        s = jnp.einsum('bqkgd,bskd->bkgqs', qblk, k).astype(jnp.float32) * scale
        p = jax.nn.softmax(s, axis=-1).astype(v.dtype)
        return jnp.einsum('bkgqs,bskd->bqkgd', p, v)

    o = lax.map(block, qb)
    o = o.transpose(1, 0, 2, 3, 4, 5).reshape(B, S, qd)
    return o @ w_out


def diff_mixer(h, w_in, lq1, lk1, lq2, lk2, sub_gain, w_out, rel_bias, lambda_init):
    B, S, _ = h.shape
    qk = DIFF_HEADS * 2 * DIFF_HEAD_DIM
    proj = h @ w_in
    q = proj[..., :qk].reshape(B, S, DIFF_HEADS, 2, DIFF_HEAD_DIM)
    k = proj[..., qk:2 * qk].reshape(B, S, DIFF_HEADS, 2, DIFF_HEAD_DIM)
    v = proj[..., 2 * qk:].reshape(B, S, DIFF_HEADS, DIFF_V_DIM)
    f32 = jnp.float32
    lam = (jnp.exp(jnp.sum(lq1.astype(f32) * lk1.astype(f32)))
           - jnp.exp(jnp.sum(lq2.astype(f32) * lk2.astype(f32))) + lambda_init)
    nblk = S // BLOCK_Q
    qb = q.reshape(B, nblk, BLOCK_Q, DIFF_HEADS, 2, DIFF_HEAD_DIM).transpose(1, 0, 2, 3, 4, 5)
    starts = jnp.arange(nblk, dtype=jnp.int32) * BLOCK_Q
    kpos = jnp.arange(S, dtype=jnp.int32)
    scale = DIFF_HEAD_DIM ** -0.5

    def block(args):
        qblk, q0 = args
        qpos = q0 + jnp.arange(BLOCK_Q, dtype=jnp.int32)
        bucket = t5_bucket(kpos[None, :] - qpos[:, None])
        bias = jnp.transpose(rel_bias[bucket].astype(f32), (2, 0, 1))
        s = jnp.einsum('bqhmd,bshmd->bhmqs', qblk, k).astype(f32) * scale + bias[None, :, None]
        p = jax.nn.softmax(s, axis=-1)
        a = (p[:, :, 0] - lam * p[:, :, 1]).astype(v.dtype)
        return jnp.einsum('bhqs,bshe->bqhe', a, v)

    o = lax.map(block, (qb, starts))
    o = o.transpose(1, 0, 2, 3, 4).reshape(B, S, DIFF_HEADS, DIFF_V_DIM)
    o = rms_norm(o, sub_gain) * (1.0 - lambda_init)
    return o.reshape(B, S, DIFF_HEADS * DIFF_V_DIM) @ w_out


def expert_choice_ffn(h, router_w, w_gate, w_up, w_down):
    B, S, D = h.shape
    E = w_gate.shape[0]
    cap = EC_CAPACITY_FACTOR * S // E
    logits = jnp.einsum('bsd,de->bse', h, router_w).astype(jnp.float32)
    aff = jax.nn.softmax(logits, axis=-1)
    gate, idx = lax.top_k(jnp.transpose(aff, (0, 2, 1)), cap)
    flat_idx = idx + (jnp.arange(B, dtype=jnp.int32) * S)[:, None, None]
    h_flat = h.reshape(B * S, D)
    xin = h_flat[flat_idx]
    g = jnp.einsum('becd,edf->becf', xin, w_gate)
    u = jnp.einsum('becd,edf->becf', xin, w_up)
    y = jnp.einsum('becf,efd->becd', jax.nn.silu(g) * u, w_down)
    y = y * gate[..., None].astype(y.dtype)
    out = jnp.zeros_like(h_flat).at[flat_idx.reshape(-1)].add(y.reshape(-1, D))
    return out.reshape(B, S, D)


def setup_inputs(seed: int = 0) -> dict:
    key = jax.random.key(seed)
    ks = jax.random.split(key, 24)
    f32 = jnp.float32
    n_a = len(range(0, DEPTH, N_MIXERS))
    n_b = DEPTH - n_a
    qd = GQA_HEADS * HEAD_DIM
    kd = GQA_KV_HEADS * HEAD_DIM
    dqk = DIFF_HEADS * 2 * DIFF_HEAD_DIM
    dv = DIFF_HEADS * DIFF_V_DIM

    def nrm(k, shape, scale):
        return jax.random.normal(k, shape, f32) * scale

    def gain(k, shape):
        return 1.0 + 0.02 * jax.random.normal(k, shape, f32)

    return {
        'x': nrm(ks[0], (BATCH, SEQ, D_MODEL), 1.0),
        'attn_norm': gain(ks[1], (DEPTH, D_MODEL)),
        'ffn_norm': gain(ks[2], (DEPTH, D_MODEL)),
        'final_norm': gain(ks[3], (D_MODEL,)),
        'gqa_w_in': nrm(ks[4], (n_a, D_MODEL, qd + 2 * kd), D_MODEL ** -0.5),
        'gqa_q_norm': gain(ks[5], (n_a, HEAD_DIM)),
        'gqa_k_norm': gain(ks[6], (n_a, HEAD_DIM)),
        'gqa_w_out': nrm(ks[7], (n_a, qd, D_MODEL), qd ** -0.5),
        'diff_w_in': nrm(ks[8], (n_b, D_MODEL, 2 * dqk + dv), D_MODEL ** -0.5),
        'diff_lambda_q1': nrm(ks[9], (n_b, DIFF_HEAD_DIM), 0.1),
        'diff_lambda_k1': nrm(ks[10], (n_b, DIFF_HEAD_DIM), 0.1),
        'diff_lambda_q2': nrm(ks[11], (n_b, DIFF_HEAD_DIM), 0.1),
        'diff_lambda_k2': nrm(ks[12], (n_b, DIFF_HEAD_DIM), 0.1),
        'diff_sub_norm': gain(ks[13], (n_b, DIFF_V_DIM)),
        'diff_w_out': nrm(ks[14], (n_b, dv, D_MODEL), dv ** -0.5),
        'rel_bias': nrm(ks[15], (NUM_BUCKETS, DIFF_HEADS), 0.2),
        'router_w': nrm(ks[16], (DEPTH, D_MODEL, N_EXPERTS), D_MODEL ** -0.5),
        'expert_w_gate': nrm(ks[17], (DEPTH, N_EXPERTS, D_MODEL, EXPERT_FF), D_MODEL ** -0.5),
        'expert_w_up': nrm(ks[18], (DEPTH, N_EXPERTS, D_MODEL, EXPERT_FF), D_MODEL ** -0.5),
        'expert_w_down': nrm(ks[19], (DEPTH, N_EXPERTS, EXPERT_FF, D_MODEL), EXPERT_FF ** -0.5),
    }


def reference(x, attn_norm, ffn_norm, final_norm, gqa_w_in, gqa_q_norm, gqa_k_norm, gqa_w_out,
              diff_w_in, diff_lambda_q1, diff_lambda_k1, diff_lambda_q2, diff_lambda_k2,
              diff_sub_norm, diff_w_out, rel_bias, router_w, expert_w_gate, expert_w_up,
              expert_w_down):
    S = x.shape[1]
    rows = S // GRID_W
    row = jnp.repeat(jnp.arange(rows, dtype=jnp.int32), GRID_W)
    col = jnp.tile(jnp.arange(GRID_W, dtype=jnp.int32), rows)
    for i in range(DEPTH):
        h = rms_norm(x, attn_norm[i])
        j = i // N_MIXERS
        if i % N_MIXERS == 0:
            x = x + gqa_axial_mixer(h, gqa_w_in[j], gqa_q_norm[j], gqa_k_norm[j], gqa_w_out[j], row, col)
        else:
            lambda_init = 0.8 - 0.6 * math.exp(-0.3 * i)
            x = x + diff_mixer(h, diff_w_in[j], diff_lambda_q1[j], diff_lambda_k1[j],
                               diff_lambda_q2[j], diff_lambda_k2[j], diff_sub_norm[j],
                               diff_w_out[j], rel_bias, lambda_init)
        h = rms_norm(x, ffn_norm[i])
        x = x + expert_choice_ffn(h, router_w[i], expert_w_gate[i], expert_w_up[i], expert_w_down[i])
    return rms_norm(x, final_norm)
```

```python
import functools
import math

import numpy as np
import jax
import jax.numpy as jnp
from jax import lax
from jax.experimental import pallas as pl
from jax.experimental.pallas import tpu as pltpu

HEAD_DIM = 128
GQA_GROUP = 4
DIFF_HEAD_DIM = 128
DIFF_V_DIM = 2 * DIFF_HEAD_DIM
EC_CAPACITY_FACTOR = 2
NUM_BUCKETS = 32
MAX_DISTANCE = 128
GRID_W = 64
ROPE_THETA = 10000.0
NORM_EPS = 1e-6
N_MIXERS = 2

LANES = 128
VMEM_LIMIT_BYTES = 56 * 1024 * 1024

NORM_ROWS = 256
MM_TM, MM_TN = 1024, 512
PREP_ROWS = 512
GQA_TQ, GQA_TK = 512, 1024
DIFF_T = 1024
ROUTE_ROWS = 512
SLOT_BLOCK = 512
FFN_KCHUNK = 1024
NEG_BIG = -1e30


def _cparams(semantics):
    return pltpu.CompilerParams(dimension_semantics=semantics, vmem_limit_bytes=VMEM_LIMIT_BYTES)


def _rmsnorm_kernel(x_ref, g_ref, o_ref):
    x = x_ref[...]
    y = x * lax.rsqrt(jnp.mean(x * x, axis=-1, keepdims=True) + NORM_EPS)
    o_ref[...] = (y * g_ref[...]).astype(o_ref.dtype)


def rmsnorm(x, g, out_dtype):
    s, d = x.shape
    return pl.pallas_call(
        _rmsnorm_kernel,
        grid=(s // NORM_ROWS,),
        in_specs=[pl.BlockSpec((NORM_ROWS, d), lambda i: (i, 0)),
                  pl.BlockSpec((1, d), lambda i: (0, 0))],
        out_specs=pl.BlockSpec((NORM_ROWS, d), lambda i: (i, 0)),
        out_shape=jax.ShapeDtypeStruct((s, d), out_dtype),
        compiler_params=_cparams(("arbitrary",)),
        name="rmsnorm",
    )(x, g.reshape(1, d))


def _mm_kernel(a_ref, w_ref, o_ref):
    o_ref[...] = jnp.dot(a_ref[...], w_ref[...], preferred_element_type=jnp.float32).astype(o_ref.dtype)


def _mm_res_kernel(a_ref, w_ref, r_ref, o_ref):
    o_ref[...] = r_ref[...] + jnp.dot(a_ref[...], w_ref[...], preferred_element_type=jnp.float32)


def matmul(a, w, out_dtype, residual=None):
    m, k = a.shape
    n = w.shape[1]
    tm, tn = min(MM_TM, m), min(MM_TN, n)
    in_specs = [pl.BlockSpec((tm, k), lambda i, j: (i, 0)),
                pl.BlockSpec((k, tn), lambda i, j: (0, j))]
    args = [a, w]
    kern = _mm_kernel
    if residual is not None:
        in_specs.append(pl.BlockSpec((tm, tn), lambda i, j: (i, j)))
        args.append(residual)
        kern = _mm_res_kernel
    return pl.pallas_call(
        kern,
        grid=(m // tm, n // tn),
        in_specs=in_specs,
        out_specs=pl.BlockSpec((tm, tn), lambda i, j: (i, j)),
        out_shape=jax.ShapeDtypeStruct((m, n), out_dtype),
        compiler_params=_cparams(("arbitrary", "arbitrary")),
        name="proj_matmul",
    )(*args)


def _rope_tables(s):
    half = HEAD_DIM // 2
    rows = s // GRID_W
    row = jnp.repeat(jnp.arange(rows, dtype=jnp.int32), GRID_W)
    col = jnp.tile(jnp.arange(GRID_W, dtype=jnp.int32), rows)
    freq = ROPE_THETA ** (-jnp.arange(0, half, 2, dtype=jnp.float32) / half)
    ang_r = row.astype(jnp.float32)[:, None] * freq[None, :]
    ang_c = col.astype(jnp.float32)[:, None] * freq[None, :]
    cos_t = jnp.concatenate([jnp.cos(ang_r), jnp.cos(ang_r), jnp.cos(ang_c), jnp.cos(ang_c)], axis=-1)
    sin_t = jnp.concatenate([-jnp.sin(ang_r), jnp.sin(ang_r), -jnp.sin(ang_c), jnp.sin(ang_c)], axis=-1)
    return cos_t, sin_t


def _gqa_prep_kernel(n_q, n_qk, p_ref, cos_ref, sin_ref, g_ref, o_ref):
    hd = pl.program_id(1)

    @pl.when(hd < n_qk)
    def _():
        t = p_ref[...]
        y = t * lax.rsqrt(jnp.mean(t * t, axis=-1, keepdims=True) + NORM_EPS)
        y = y * g_ref[0]
        quarter = HEAD_DIM // 4
        lane = lax.broadcasted_iota(jnp.int32, y.shape, 1)
        first = (lane & (2 * quarter - 1)) < quarter
        partner = jnp.where(first,
                            pltpu.roll(y, HEAD_DIM - quarter, axis=1),
                            pltpu.roll(y, quarter, axis=1))
        r = y * cos_ref[...] + partner * sin_ref[...]
        scale = jnp.where(hd < n_q, HEAD_DIM ** -0.5, 1.0).astype(jnp.float32)
        o_ref[...] = (r * scale).astype(o_ref.dtype)

    @pl.when(hd >= n_qk)
    def _():
        o_ref[...] = p_ref[...].astype(o_ref.dtype)


def gqa_prep(proj, q_gain, k_gain, n_q, n_kv):
    s, width = proj.shape
    n_heads = width // HEAD_DIM
    n_qk = n_q + n_kv
    cos_t, sin_t = _rope_tables(s)
    gains = jnp.concatenate([jnp.tile(q_gain[None], (n_q, 1)), jnp.tile(k_gain[None], (n_kv, 1)),
                             jnp.ones((n_kv, HEAD_DIM), jnp.float32)], axis=0).reshape(n_heads, 1, HEAD_DIM)
    ts = PREP_ROWS
    return pl.pallas_call(
        functools.partial(_gqa_prep_kernel, n_q, n_qk),
        grid=(s // ts, n_heads),
        in_specs=[pl.BlockSpec((ts, HEAD_DIM), lambda i, h: (i, h)),
                  pl.BlockSpec((ts, HEAD_DIM), lambda i, h: (i, 0)),
                  pl.BlockSpec((ts, HEAD_DIM), lambda i, h: (i, 0)),
                  pl.BlockSpec((1, 1, HEAD_DIM), lambda i, h: (h, 0, 0))],
        out_specs=pl.BlockSpec((ts, HEAD_DIM), lambda i, h: (i, h)),
        out_shape=jax.ShapeDtypeStruct((s, width), jnp.bfloat16),
        compiler_params=_cparams(("arbitrary", "arbitrary")),
        name="gqa_prep",
    )(proj, cos_t, sin_t, gains)


def _gqa_attn_kernel(q_ref, k_ref, v_ref, o_ref, m_sc, l_sc, acc_sc):
    ki = pl.program_id(2)
    last = pl.num_programs(2) - 1

    @pl.when(ki == 0)
    def _():
        m_sc[...] = jnp.full(m_sc.shape, NEG_BIG, jnp.float32)
        l_sc[...] = jnp.zeros(l_sc.shape, jnp.float32)
        acc_sc[...] = jnp.zeros(acc_sc.shape, jnp.float32)

    k = k_ref[...]
    v = v_ref[...]
    for g in range(GQA_GROUP):
        q = q_ref[:, g * HEAD_DIM:(g + 1) * HEAD_DIM]
        s = lax.dot_general(q, k, (((1,), (1,)), ((), ())), preferred_element_type=jnp.float32)
        m_prev = m_sc[g]
        m_new = jnp.maximum(m_prev, jnp.max(s, axis=1, keepdims=True))
        alpha = jnp.exp(m_prev - m_new)
        p = jnp.exp(s - m_new)
        l_sc[g] = alpha * l_sc[g] + jnp.sum(p, axis=1, keepdims=True)
        acc_sc[g] = alpha * acc_sc[g] + jnp.dot(p.astype(v.dtype), v, preferred_element_type=jnp.float32)
        m_sc[g] = m_new

    @pl.when(ki == last)
    def _():
        for g in range(GQA_GROUP):
            o_ref[:, g * HEAD_DIM:(g + 1) * HEAD_DIM] = (acc_sc[g] / l_sc[g]).astype(o_ref.dtype)


def gqa_attention(qkv, n_q, n_kv):
    s = qkv.shape[0]
    tq, tk = min(GQA_TQ, s), min(GQA_TK, s)
    gw = GQA_GROUP * HEAD_DIM
    return pl.pallas_call(
        _gqa_attn_kernel,
        grid=(n_kv, s // tq, s // tk),
        in_specs=[pl.BlockSpec((tq, gw), lambda h, i, j: (i, h)),
                  pl.BlockSpec((tk, HEAD_DIM), lambda h, i, j: (j, n_q + h)),
                  pl.BlockSpec((tk, HEAD_DIM), lambda h, i, j: (j, n_q + n_kv + h))],
        out_specs=pl.BlockSpec((tq, gw), lambda h, i, j: (i, h)),
        out_shape=jax.ShapeDtypeStruct((s, n_q * HEAD_DIM), jnp.bfloat16),
        scratch_shapes=[pltpu.VMEM((GQA_GROUP, tq, 1), jnp.float32),
                        pltpu.VMEM((GQA_GROUP, tq, 1), jnp.float32),
                        pltpu.VMEM((GQA_GROUP, tq, HEAD_DIM), jnp.float32)],
        compiler_params=_cparams(("arbitrary", "arbitrary", "arbitrary")),
        name="gqa_attention",
    )(qkv, qkv, qkv)


def _t5_bucket(rel):
    nb = NUM_BUCKETS // 2
    max_exact = nb // 2
    n = jnp.abs(rel)
    large = max_exact + (jnp.log(jnp.maximum(n, 1).astype(jnp.float32) / max_exact)
                         / math.log(MAX_DISTANCE / max_exact) * (nb - max_exact)).astype(jnp.int32)
    large = jnp.minimum(large, nb - 1)
    return jnp.where(rel > 0, nb, 0) + jnp.where(n < max_exact, n, large)


def _bucket_saturation_distance():
    nb = NUM_BUCKETS // 2
    max_exact = nb // 2
    n = np.arange(1, 4 * MAX_DISTANCE, dtype=np.float64)
    large = max_exact + np.floor(np.log(n / max_exact) / math.log(MAX_DISTANCE / max_exact) * (nb - max_exact))
    unsat = np.nonzero(np.minimum(large, nb - 1) < nb - 1)[0]
    return int(n[unsat[-1]]) + 2


def _bias_tile_kernel(t, rows, rb_ref, o_ref):
    h = pl.program_id(0)
    d = pl.program_id(1)
    offset = (d - 1) * t

    def body(c, carry):
        r0 = pl.multiple_of(c * rows, rows)
        i = lax.broadcasted_iota(jnp.int32, (rows, t), 0) + r0
        j = lax.broadcasted_iota(jnp.int32, (rows, t), 1)
        bucket = _t5_bucket(j - i + offset)
        bias = jnp.zeros((rows, t), jnp.float32)
        for b in range(NUM_BUCKETS):
            bias = jnp.where(bucket == b, rb_ref[b, h], bias)
        o_ref[0, 0, pl.ds(r0, rows), :] = bias
        return carry

    lax.fori_loop(0, t // rows, body, 0)


def bias_tiles(rel_bias, t):
    n_heads = rel_bias.shape[1]
    rows = min(128, t)
    return pl.pallas_call(
        functools.partial(_bias_tile_kernel, t, rows),
        grid=(n_heads, 3),
        in_specs=[pl.BlockSpec(memory_space=pltpu.SMEM)],
        out_specs=pl.BlockSpec((1, 1, t, t), lambda h, d: (h, d, 0, 0)),
        out_shape=jax.ShapeDtypeStruct((n_heads, 3, t, t), jnp.float32),
        compiler_params=_cparams(("arbitrary", "arbitrary")),
        name="t5_bias_tiles",
    )(rel_bias)


def _diff_attn_kernel(lambda_init, q_ref, k_ref, v_ref, b_ref, rb_ref, lam_ref, sg_ref, o_ref,
                      m_sc, l_sc, acc_sc):
    h = pl.program_id(0)
    qi = pl.program_id(1)
    ki = pl.program_id(2)
    last = pl.num_programs(2) - 1
    nb = NUM_BUCKETS // 2
    scale = DIFF_HEAD_DIM ** -0.5

    @pl.when(ki == 0)
    def _():
        m_sc[...] = jnp.full(m_sc.shape, NEG_BIG, jnp.float32)
        l_sc[...] = jnp.zeros(l_sc.shape, jnp.float32)
        acc_sc[...] = jnp.zeros(acc_sc.shape, jnp.float32)

    def update(bias):
        v = v_ref[...]
        for mp in range(2):
            q = q_ref[:, mp * DIFF_HEAD_DIM:(mp + 1) * DIFF_HEAD_DIM]
            k = k_ref[:, mp * DIFF_HEAD_DIM:(mp + 1) * DIFF_HEAD_DIM]
            s = lax.dot_general(q, k, (((1,), (1,)), ((), ())), preferred_element_type=jnp.float32)
            s = s * scale + bias
            m_prev = m_sc[mp]
            m_new = jnp.maximum(m_prev, jnp.max(s, axis=1, keepdims=True))
            alpha = jnp.exp(m_prev - m_new)
            p = jnp.exp(s - m_new)
            l_sc[mp] = alpha * l_sc[mp] + jnp.sum(p, axis=1, keepdims=True)
            acc_sc[mp] = alpha * acc_sc[mp] + jnp.dot(p.astype(v.dtype), v, preferred_element_type=jnp.float32)
            m_sc[mp] = m_new

    near = jnp.abs(ki - qi) <= 1

    @pl.when(near)
    def _():
        update(b_ref[0, 0])

    @pl.when(jnp.logical_not(near))
    def _():
        update(jnp.where(ki < qi, rb_ref[nb - 1, h], rb_ref[NUM_BUCKETS - 1, h]))

    @pl.when(ki == last)
    def _():
        lv = lam_ref[...]
        lam = (jnp.exp(jnp.sum(lv[0:1] * lv[1:2], axis=1, keepdims=True))
               - jnp.exp(jnp.sum(lv[2:3] * lv[3:4], axis=1, keepdims=True)) + lambda_init)
        o = acc_sc[0] / l_sc[0] - lam * (acc_sc[1] / l_sc[1])
        y = o * lax.rsqrt(jnp.mean(o * o, axis=-1, keepdims=True) + NORM_EPS)
        o_ref[...] = ((y * sg_ref[...]) * (1.0 - lambda_init)).astype(o_ref.dtype)


def diff_attention(proj, btiles, rel_bias, lam_vecs, sub_gain, lambda_init, n_heads):
    s = proj.shape[0]
    t = btiles.shape[-1]
    assert t + 1 >= _bucket_saturation_distance()
    w = DIFF_V_DIM
    return pl.pallas_call(
        functools.partial(_diff_attn_kernel, lambda_init),
        grid=(n_heads, s // t, s // t),
        in_specs=[pl.BlockSpec((t, w), lambda h, i, j: (i, h)),
                  pl.BlockSpec((t, w), lambda h, i, j: (j, n_heads + h)),
                  pl.BlockSpec((t, w), lambda h, i, j: (j, 2 * n_heads + h)),
                  pl.BlockSpec((1, 1, t, t), lambda h, i, j: (h, jnp.clip(j - i + 1, 0, 2), 0, 0)),
                  pl.BlockSpec(memory_space=pltpu.SMEM),
                  pl.BlockSpec((4, DIFF_HEAD_DIM), lambda h, i, j: (0, 0)),
                  pl.BlockSpec((1, w), lambda h, i, j: (0, 0))],
        out_specs=pl.BlockSpec((t, w), lambda h, i, j: (i, h)),
        out_shape=jax.ShapeDtypeStruct((s, n_heads * w), jnp.bfloat16),
        scratch_shapes=[pltpu.VMEM((2, t, 1), jnp.float32),
                        pltpu.VMEM((2, t, 1), jnp.float32),
                        pltpu.VMEM((2, t, w), jnp.float32)],
        compiler_params=_cparams(("arbitrary", "arbitrary", "arbitrary")),
        name="diff_attention",
    )(proj, proj, proj, btiles, rel_bias, lam_vecs, sub_gain.reshape(1, w))


def _route_kernel(x_ref, g_ref, rw_ref, h_ref, aff_ref):
    x = x_ref[...]
    y = x * lax.rsqrt(jnp.mean(x * x, axis=-1, keepdims=True) + NORM_EPS)
    hn = y * g_ref[...]
    h_ref[...] = hn
    logits = lax.dot_general(rw_ref[...], hn.astype(jnp.bfloat16), (((1,), (1,)), ((), ())),
                             preferred_element_type=jnp.float32)
    e = jnp.exp(logits - jnp.max(logits, axis=0, keepdims=True))
    aff_ref[...] = e / jnp.sum(e, axis=0, keepdims=True)


def route(x, g, router_w_t):
    s, d = x.shape
    n_e = router_w_t.shape[0]
    ts = min(ROUTE_ROWS, s)
    return pl.pallas_call(
        _route_kernel,
        grid=(s // ts,),
        in_specs=[pl.BlockSpec((ts, d), lambda i: (i, 0)),
                  pl.BlockSpec((1, d), lambda i: (0, 0)),
                  pl.BlockSpec((n_e, d), lambda i: (0, 0))],
        out_specs=[pl.BlockSpec((ts, d), lambda i: (i, 0)),
                   pl.BlockSpec((n_e, ts), lambda i: (0, i))],
        out_shape=[jax.ShapeDtypeStruct((s, d), jnp.float32),
                   jax.ShapeDtypeStruct((n_e, s), jnp.float32)],
        compiler_params=_cparams(("arbitrary",)),
        name="moe_route",
    )(x, g.reshape(1, d), router_w_t)


def _split3_bf16(x):
    hi = x.astype(jnp.bfloat16)
    r1 = x - hi.astype(jnp.float32)
    mid = r1.astype(jnp.bfloat16)
    lo = (r1 - mid.astype(jnp.float32)).astype(jnp.bfloat16)
    return hi, mid, lo


def _select_kernel(cap, aff_ref, idx_ref, gate_ref):
    a = aff_ref[0]
    nch = a.shape[0]
    bits = pltpu.bitcast(a, jnp.int32)

    def count(mask):
        c = jnp.sum(mask.astype(jnp.float32), axis=1, keepdims=True)
        return jnp.sum(c, axis=0, keepdims=True)

    thr = jnp.zeros((1, 1), jnp.int32)
    for b in range(30, -1, -1):
        cand = thr | (1 << b)
        thr = jnp.where(count(bits >= cand) >= cap, cand, thr)

    li = lax.broadcasted_iota(jnp.int32, (LANES, LANES), 0)
    lj = lax.broadcasted_iota(jnp.int32, (LANES, LANES), 1)
    tri_incl = (li <= lj).astype(jnp.bfloat16)
    ci = lax.broadcasted_iota(jnp.int32, (nch, nch), 0)
    cj = lax.broadcasted_iota(jnp.int32, (nch, nch), 1)
    before_rows = (cj < ci).astype(jnp.bfloat16)
    before_cols = (ci < cj).astype(jnp.bfloat16)
    ones8 = jnp.ones((8, LANES), jnp.bfloat16)

    def prefix(mask_f32):
        mb = mask_f32.astype(jnp.bfloat16)
        incl = jnp.dot(mb, tri_incl, preferred_element_type=jnp.float32)
        start_col = jnp.sum(jnp.dot(before_rows, mb, preferred_element_type=jnp.float32),
                            axis=1, keepdims=True)
        tot = lax.dot_general(ones8, mb, (((1,), (1,)), ((), ())), preferred_element_type=jnp.float32)
        start_lane = jnp.dot(tot.astype(jnp.bfloat16), before_cols,
                             preferred_element_type=jnp.float32)[0:1]
        return incl, start_col, start_lane

    gt = bits > thr
    eq = bits == thr
    need = cap - count(gt)
    eq_f = eq.astype(jnp.float32)
    incl_eq, start_eq, _ = prefix(eq_f)
    keep = jnp.logical_and(eq, (incl_eq - eq_f + start_eq) < need)
    sel_f = jnp.logical_or(gt, keep).astype(jnp.float32)
    incl, _, start_lane = prefix(sel_f)

    slot = lax.broadcasted_iota(jnp.int32, (cap, 1), 0).astype(jnp.float32)
    chunk = jnp.sum((start_lane <= slot).astype(jnp.float32), axis=1, keepdims=True) - 1.0
    chunk_iota = lax.broadcasted_iota(jnp.int32, (cap, nch), 1).astype(jnp.float32)
    onehot = chunk_iota == chunk
    chunk_start = jnp.sum(jnp.where(onehot, start_lane, 0.0), axis=1, keepdims=True)
    onehot_b = onehot.astype(jnp.bfloat16)
    rank = slot - chunk_start
    incl_rows = jnp.dot(onehot_b, incl.astype(jnp.bfloat16), preferred_element_type=jnp.float32)
    lane = jnp.sum((incl_rows <= rank).astype(jnp.float32), axis=1, keepdims=True)
    idx_ref[0] = (chunk * LANES + lane).astype(jnp.int32)

    hi, mid, lo = _split3_bf16(a)
    arow = (jnp.dot(onehot_b, hi, preferred_element_type=jnp.float32)
            + jnp.dot(onehot_b, mid, preferred_element_type=jnp.float32)
            + jnp.dot(onehot_b, lo, preferred_element_type=jnp.float32))
    lane_iota = lax.broadcasted_iota(jnp.int32, (cap, LANES), 1).astype(jnp.float32)
    gate_ref[0] = jnp.sum(jnp.where(lane_iota == lane, arow, 0.0), axis=1, keepdims=True)


def select(aff_t, cap):
    n_e, s = aff_t.shape
    nch = s // LANES
    return pl.pallas_call(
        functools.partial(_select_kernel, cap),
        grid=(n_e,),
        in_specs=[pl.BlockSpec((1, nch, LANES), lambda e: (e, 0, 0))],
        out_specs=[pl.BlockSpec((1, cap, 1), lambda e: (e, 0, 0)),
                   pl.BlockSpec((1, cap, 1), lambda e: (e, 0, 0))],
        out_shape=[jax.ShapeDtypeStruct((n_e, cap, 1), jnp.int32),
                   jax.ShapeDtypeStruct((n_e, cap, 1), jnp.float32)],
        compiler_params=_cparams(("arbitrary",)),
        name="moe_select",
    )(aff_t.reshape(n_e, nch, LANES))


def _row_copies(src_hbm, dst_vmem, sem, idx_ref, base, n_rows, to_vmem):
    def copy(r):
        tok = idx_ref[base + r]
        hbm_row = src_hbm.at[pl.ds(tok, 1)]
        vmem_row = dst_vmem.at[pl.ds(r, 1)]
        if to_vmem:
            return pltpu.make_async_copy(hbm_row, vmem_row, sem)
        return pltpu.make_async_copy(vmem_row, hbm_row, sem)

    def start(r, c):
        copy(r).start()
        return c

    def wait(r, c):
        copy(r).wait()
        return c

    lax.fori_loop(0, n_rows, start, 0)
    lax.fori_loop(0, n_rows, wait, 0)


def _expert_kernel(n_k, blocks_per_e, idx_ref, h_hbm, gate_ref, wg_ref, wu_ref, wd_ref, y_ref,
                   xs, g_acc, u_acc, act, sem):
    blk = pl.program_id(0)
    j = pl.program_id(1)
    rows = xs.shape[0]
    kc = wg_ref.shape[1]

    @pl.when(j == 0)
    def _():
        _row_copies(h_hbm, xs, sem, idx_ref, blk * rows, rows, True)
        g_acc[...] = jnp.zeros(g_acc.shape, jnp.float32)
        u_acc[...] = jnp.zeros(u_acc.shape, jnp.float32)

    for jj in range(n_k):
        @pl.when(j == jj)
        def _():
            xk = xs[:, jj * kc:(jj + 1) * kc].astype(jnp.bfloat16)
            g_acc[...] += jnp.dot(xk, wg_ref[0], preferred_element_type=jnp.float32)
            u_acc[...] += jnp.dot(xk, wu_ref[0], preferred_element_type=jnp.float32)

    @pl.when(j == n_k - 1)
    def _():
        g = g_acc[...]
        act[...] = ((g * (1.0 / (1.0 + jnp.exp(-g)))) * u_acc[...]).astype(act.dtype)

    @pl.when(j >= n_k)
    def _():
        y = jnp.dot(act[...], wd_ref[0], preferred_element_type=jnp.float32)
        y_ref[...] = y * gate_ref[...]


def expert_ffn(h, idx_flat, gate_col, w_gate, w_up, w_down):
    s, d = h.shape
    n_e, _, ff = w_gate.shape
    n_slots = idx_flat.shape[0]
    cap = n_slots // n_e
    rows = min(SLOT_BLOCK, cap)
    blocks_per_e = cap // rows
    kc = min(FFN_KCHUNK, d)
    n_k = d // kc
    n_n = d // kc
    grid_spec = pltpu.PrefetchScalarGridSpec(
        num_scalar_prefetch=1,
        grid=(n_slots // rows, n_k + n_n),
        in_specs=[pl.BlockSpec(memory_space=pl.ANY),
                  pl.BlockSpec((rows, 1), lambda b, j, idx: (b, 0)),
                  pl.BlockSpec((1, kc, ff), lambda b, j, idx: (b // blocks_per_e, jnp.minimum(j, n_k - 1), 0)),
                  pl.BlockSpec((1, kc, ff), lambda b, j, idx: (b // blocks_per_e, jnp.minimum(j, n_k - 1), 0)),
                  pl.BlockSpec((1, ff, kc), lambda b, j, idx: (b // blocks_per_e, 0, jnp.maximum(j - n_k, 0)))],
        out_specs=pl.BlockSpec((rows, kc), lambda b, j, idx: (b, jnp.maximum(j - n_k, 0))),
        scratch_shapes=[pltpu.VMEM((rows, d), jnp.float32),
                        pltpu.VMEM((rows, ff), jnp.float32),
                        pltpu.VMEM((rows, ff), jnp.float32),
                        pltpu.VMEM((rows, ff), jnp.bfloat16),
                        pltpu.SemaphoreType.DMA(())],
    )
    return pl.pallas_call(
        functools.partial(_expert_kernel, n_k, blocks_per_e),
        grid_spec=grid_spec,
        out_shape=jax.ShapeDtypeStruct((n_slots, d), jnp.float32),
        compiler_params=_cparams(("arbitrary", "arbitrary")),
        name="moe_expert_ffn",
    )(idx_flat, h, gate_col, w_gate, w_up, w_down)


def _combine_kernel(idx_ref, x_in_hbm, y_ref, x_hbm, stage, sem_in, sem_out):
    del x_in_hbm
    blk = pl.program_id(0)
    rows = stage.shape[0]
    _row_copies(x_hbm, stage, sem_in, idx_ref, blk * rows, rows, True)
    stage[...] += y_ref[...]
    _row_copies(x_hbm, stage, sem_out, idx_ref, blk * rows, rows, False)


def combine(x, y_slots, idx_flat, n_e):
    s, d = x.shape
    n_slots = idx_flat.shape[0]
    cap = n_slots // n_e
    rows = min(SLOT_BLOCK, cap)
    grid_spec = pltpu.PrefetchScalarGridSpec(
        num_scalar_prefetch=1,
        grid=(n_slots // rows,),
        in_specs=[pl.BlockSpec(memory_space=pl.ANY),
                  pl.BlockSpec((rows, d), lambda b, idx: (b, 0))],
        out_specs=pl.BlockSpec(memory_space=pl.ANY),
        scratch_shapes=[pltpu.VMEM((rows, d), jnp.float32),
                        pltpu.SemaphoreType.DMA(()),
                        pltpu.SemaphoreType.DMA(())],
    )
    return pl.pallas_call(
        _combine_kernel,
        grid_spec=grid_spec,
        out_shape=jax.ShapeDtypeStruct((s, d), jnp.float32),
        input_output_aliases={1: 0},
        compiler_params=_cparams(("arbitrary",)),
        name="moe_combine",
    )(idx_flat, x, y_slots)


def expert_choice_ffn(x, norm_g, router_w, w_gate, w_up, w_down):
    s, d = x.shape
    n_e = w_gate.shape[0]
    cap = EC_CAPACITY_FACTOR * s // n_e
    bf = jnp.bfloat16
    h, aff_t = route(x, norm_g, router_w.T.astype(bf))
    idx, gate = select(aff_t, cap)
    idx_flat = idx.reshape(n_e * cap)
    y = expert_ffn(h, idx_flat, gate.reshape(n_e * cap, 1), w_gate.astype(bf), w_up.astype(bf), w_down.astype(bf))
    return combine(x, y, idx_flat, n_e)


def kernel(x, attn_norm, ffn_norm, final_norm, gqa_w_in, gqa_q_norm, gqa_k_norm, gqa_w_out,
           diff_w_in, diff_lambda_q1, diff_lambda_k1, diff_lambda_q2, diff_lambda_k2,
           diff_sub_norm, diff_w_out, rel_bias, router_w, expert_w_gate, expert_w_up,
           expert_w_down):
    b, s, d = x.shape
    assert b == 1
    bf = jnp.bfloat16
    depth = attn_norm.shape[0]
    n_q = d // HEAD_DIM
    n_kv = n_q // GQA_GROUP
    n_diff = d // DIFF_V_DIM
    diff_t = min(DIFF_T, s)
    btiles = bias_tiles(rel_bias, diff_t) if depth > 1 else None

    xs = x.reshape(s, d)
    for i in range(depth):
        h = rmsnorm(xs, attn_norm[i], bf)
        j = i // N_MIXERS
        if i % N_MIXERS == 0:
            proj = matmul(h, gqa_w_in[j].astype(bf), jnp.float32)
            qkv = gqa_prep(proj, gqa_q_norm[j], gqa_k_norm[j], n_q, n_kv)
            o = gqa_attention(qkv, n_q, n_kv)
            xs = matmul(o, gqa_w_out[j].astype(bf), jnp.float32, residual=xs)
        else:
            lambda_init = 0.8 - 0.6 * math.exp(-0.3 * i)
            proj = matmul(h, diff_w_in[j].astype(bf), bf)
            lam_vecs = jnp.stack([diff_lambda_q1[j], diff_lambda_k1[j], diff_lambda_q2[j], diff_lambda_k2[j]])
            o = diff_attention(proj, btiles, rel_bias, lam_vecs, diff_sub_norm[j], lambda_init, n_diff)
            xs = matmul(o, diff_w_out[j].astype(bf), jnp.float32, residual=xs)
        xs = expert_choice_ffn(xs, ffn_norm[i], router_w[i], expert_w_gate[i], expert_w_up[i], expert_w_down[i])
    return rmsnorm(xs, final_norm, jnp.float32).reshape(b, s, d)
```

```python
import functools
import math

import numpy as np
import jax
import jax.numpy as jnp
from jax import lax
from jax.experimental import pallas as pl
from jax.experimental.pallas import tpu as pltpu

HEAD_DIM = 128
GQA_GROUP = 4
DIFF_HEAD_DIM = 128
DIFF_V_DIM = 2 * DIFF_HEAD_DIM
EC_CAPACITY_FACTOR = 2
NUM_BUCKETS = 32
MAX_DISTANCE = 128
GRID_W = 64
ROPE_THETA = 10000.0
NORM_EPS = 1e-6
N_MIXERS = 2

LANES = 128
VMEM_LIMIT_BYTES = 56 * 1024 * 1024

NORM_ROWS = 256
MM_TM, MM_TN = 1024, 512
PREP_ROWS = 1024
GQA_TQ, GQA_TK = 512, 2048
DIFF_T = 1024
KEY_SPLIT = 2
SCORE_LOOKAHEAD = 1
ROUTE_ROWS = 512
FFN_ROWS = 1024
FFN_KCHUNK = 512
COMBINE_ROWS = 512
NEG_BIG = -1e30
LOG2E = 1.4426950408889634


def _cparams(semantics):
    return pltpu.CompilerParams(dimension_semantics=semantics, vmem_limit_bytes=VMEM_LIMIT_BYTES)


def _rmsnorm_kernel(with_transpose, x_ref, g_ref, o_ref, *maybe_ot_ref):
    x = x_ref[...]
    y = x * lax.rsqrt(jnp.mean(x * x, axis=-1, keepdims=True) + NORM_EPS)
    h = y * g_ref[...]
    o_ref[...] = h.astype(o_ref.dtype)
    if with_transpose:
        ot_ref, = maybe_ot_ref
        ot_ref[...] = h.T.astype(ot_ref.dtype)


def rmsnorm(x, g, out_dtype, with_transpose=False):
    s, d = x.shape
    out_specs = [pl.BlockSpec((NORM_ROWS, d), lambda i: (i, 0))]
    out_shape = [jax.ShapeDtypeStruct((s, d), out_dtype)]
    if with_transpose:
        out_specs.append(pl.BlockSpec((d, NORM_ROWS), lambda i: (0, i)))
        out_shape.append(jax.ShapeDtypeStruct((d, s), out_dtype))
    out = pl.pallas_call(
        functools.partial(_rmsnorm_kernel, with_transpose),
        grid=(s // NORM_ROWS,),
        in_specs=[pl.BlockSpec((NORM_ROWS, d), lambda i: (i, 0)),
                  pl.BlockSpec((1, d), lambda i: (0, 0))],
        out_specs=out_specs,
        out_shape=out_shape,
        compiler_params=_cparams(("arbitrary",)),
        name="rmsnorm",
    )(x, g.reshape(1, d))
    return out if with_transpose else out[0]


def _mm_kernel(row_scale, has_residual, a_ref, w_ref, *rest):
    o_ref = rest[-1]
    acc = jnp.dot(a_ref[...].astype(jnp.bfloat16), w_ref[...].astype(jnp.bfloat16),
                  preferred_element_type=jnp.float32)
    if row_scale is not None:
        n_tiles, factor = row_scale
        acc = acc * jnp.where(pl.program_id(0) < n_tiles, factor, 1.0).astype(jnp.float32)
    if has_residual:
        acc = rest[0][...] + acc
    o_ref[...] = acc.astype(o_ref.dtype)


def matmul(a, w, out_dtype, residual=None, w_col_offset=0, n=None, row_scale=None):
    m, k = a.shape
    n = w.shape[1] - w_col_offset if n is None else n
    tm, tn = min(MM_TM, m), min(MM_TN, n)
    assert w_col_offset % tn == 0
    off = w_col_offset // tn
    in_specs = [pl.BlockSpec((tm, k), lambda i, j: (i, 0)),
                pl.BlockSpec((k, tn), lambda i, j: (0, j + off))]
    args = [a, w]
    if residual is not None:
        in_specs.append(pl.BlockSpec((tm, tn), lambda i, j: (i, j)))
        args.append(residual)
    if row_scale is not None:
        assert row_scale[0] % tm == 0
        row_scale = (row_scale[0] // tm, row_scale[1])
    return pl.pallas_call(
        functools.partial(_mm_kernel, row_scale, residual is not None),
        grid=(m // tm, n // tn),
        in_specs=in_specs,
        out_specs=pl.BlockSpec((tm, tn), lambda i, j: (i, j)),
        out_shape=jax.ShapeDtypeStruct((m, n), out_dtype),
        compiler_params=_cparams(("arbitrary", "arbitrary")),
        name="proj_matmul",
    )(*args)


def _rope_tables(s):
    half = HEAD_DIM // 2
    rows = s // GRID_W
    row = jnp.repeat(jnp.arange(rows, dtype=jnp.int32), GRID_W)
    col = jnp.tile(jnp.arange(GRID_W, dtype=jnp.int32), rows)
    freq = ROPE_THETA ** (-jnp.arange(0, half, 2, dtype=jnp.float32) / half)
    ang_r = row.astype(jnp.float32)[:, None] * freq[None, :]
    ang_c = col.astype(jnp.float32)[:, None] * freq[None, :]
    cos_t = jnp.concatenate([jnp.cos(ang_r), jnp.cos(ang_r), jnp.cos(ang_c), jnp.cos(ang_c)], axis=-1)
    sin_t = jnp.concatenate([-jnp.sin(ang_r), jnp.sin(ang_r), -jnp.sin(ang_c), jnp.sin(ang_c)], axis=-1)
    return cos_t, sin_t


def _head_prep_kernel(rope, transpose, scale, p_ref, *refs):
    o_ref = refs[-1]
    r = p_ref[...]
    if rope:
        cos_ref, sin_ref, g_ref = refs[:3]
        y = r * lax.rsqrt(jnp.mean(r * r, axis=-1, keepdims=True) + NORM_EPS)
        y = y * g_ref[...]
        quarter = HEAD_DIM // 4
        lane = lax.broadcasted_iota(jnp.int32, y.shape, 1)
        first = (lane & (2 * quarter - 1)) < quarter
        partner = jnp.where(first,
                            pltpu.roll(y, HEAD_DIM - quarter, axis=1),
                            pltpu.roll(y, quarter, axis=1))
        r = y * cos_ref[...] + partner * sin_ref[...]
    if scale is not None:
        r = r * scale
    if transpose:
        o_ref[0] = r.T.astype(o_ref.dtype)
    else:
        o_ref[...] = r.astype(o_ref.dtype)


def head_prep(proj, first_head, n_heads, tables=None, gain=None, scale=None, transpose=False):
    s = proj.shape[0]
    ts = min(PREP_ROWS, s)
    rope = tables is not None
    in_specs = [pl.BlockSpec((ts, HEAD_DIM), lambda i, h: (i, first_head + h))]
    args = [proj]
    if rope:
        in_specs += [pl.BlockSpec((ts, HEAD_DIM), lambda i, h: (i, 0)),
                     pl.BlockSpec((ts, HEAD_DIM), lambda i, h: (i, 0)),
                     pl.BlockSpec((1, HEAD_DIM), lambda i, h: (0, 0))]
        args += [tables[0], tables[1], gain.reshape(1, HEAD_DIM)]
    if transpose:
        out_spec = pl.BlockSpec((1, HEAD_DIM, ts), lambda i, h: (h, 0, i))
        out_shape = jax.ShapeDtypeStruct((n_heads, HEAD_DIM, s), jnp.bfloat16)
    else:
        out_spec = pl.BlockSpec((ts, HEAD_DIM), lambda i, h: (i, h))
        out_shape = jax.ShapeDtypeStruct((s, n_heads * HEAD_DIM), jnp.bfloat16)
    return pl.pallas_call(
        functools.partial(_head_prep_kernel, rope, transpose, scale),
        grid=(s // ts, n_heads),
        in_specs=in_specs,
        out_specs=out_spec,
        out_shape=out_shape,
        compiler_params=_cparams(("arbitrary", "arbitrary")),
        name="head_prep",
    )(*args)


def _gqa_attn_kernel(qt_ref, k_ref, vt_ref, o_ref, m_sc, l_sc, acc_sc):
    ki = pl.program_id(2)
    last = pl.num_programs(2) - 1

    @pl.when(ki == 0)
    def _():
        m_sc[...] = jnp.full(m_sc.shape, NEG_BIG, jnp.float32)
        l_sc[...] = jnp.zeros(l_sc.shape, jnp.float32)
        acc_sc[...] = jnp.zeros(acc_sc.shape, jnp.float32)

    tk = k_ref.shape[0]
    kc = tk // KEY_SPLIT
    stages = [(g, c) for g in range(GQA_GROUP) for c in range(KEY_SPLIT)]

    def scores(stage):
        g, c = stage
        return jnp.dot(k_ref[c * kc:(c + 1) * kc, :], qt_ref[g], preferred_element_type=jnp.float32)

    pending = [scores(stage) for stage in stages[:SCORE_LOOKAHEAD]]
    for n, (g, c) in enumerate(stages):
        if n + SCORE_LOOKAHEAD < len(stages):
            pending.append(scores(stages[n + SCORE_LOOKAHEAD]))
        st = pending.pop(0)
        m_prev = m_sc[g]
        m_new = jnp.maximum(m_prev, jnp.max(st, axis=0, keepdims=True))
        alpha = jnp.exp2(m_prev - m_new)
        p = jnp.exp2(st - m_new)
        l_sc[g] = alpha * l_sc[g] + jnp.sum(p, axis=0, keepdims=True)
        vt = vt_ref[0, :, c * kc:(c + 1) * kc]
        acc_sc[g] = alpha * acc_sc[g] + jnp.dot(vt, p.astype(vt.dtype), preferred_element_type=jnp.float32)
        m_sc[g] = m_new

    @pl.when(ki == last)
    def _():
        for g in range(GQA_GROUP):
            o_ref[:, g * HEAD_DIM:(g + 1) * HEAD_DIM] = (acc_sc[g] / l_sc[g]).T.astype(o_ref.dtype)


def gqa_attention(qt, k, vt):
    n_q, _, s = qt.shape
    n_kv = vt.shape[0]
    tq, tk = min(GQA_TQ, s), min(GQA_TK, s)
    gw = GQA_GROUP * HEAD_DIM
    return pl.pallas_call(
        _gqa_attn_kernel,
        grid=(n_kv, s // tq, s // tk),
        in_specs=[pl.BlockSpec((GQA_GROUP, HEAD_DIM, tq), lambda h, i, j: (h, 0, i)),
                  pl.BlockSpec((tk, HEAD_DIM), lambda h, i, j: (j, h)),
                  pl.BlockSpec((1, HEAD_DIM, tk), lambda h, i, j: (h, 0, j))],
        out_specs=pl.BlockSpec((tq, gw), lambda h, i, j: (i, h)),
        out_shape=jax.ShapeDtypeStruct((s, n_q * HEAD_DIM), jnp.bfloat16),
        scratch_shapes=[pltpu.VMEM((GQA_GROUP, 1, tq), jnp.float32),
                        pltpu.VMEM((GQA_GROUP, 1, tq), jnp.float32),
                        pltpu.VMEM((GQA_GROUP, HEAD_DIM, tq), jnp.float32)],
        compiler_params=_cparams(("arbitrary", "arbitrary", "arbitrary")),
        name="gqa_attention",
    )(qt, k, vt)


def _t5_bucket(rel):
    nb = NUM_BUCKETS // 2
    max_exact = nb // 2
    n = jnp.abs(rel)
    large = max_exact + (jnp.log(jnp.maximum(n, 1).astype(jnp.float32) / max_exact)
                         / math.log(MAX_DISTANCE / max_exact) * (nb - max_exact)).astype(jnp.int32)
    large = jnp.minimum(large, nb - 1)
    return jnp.where(rel > 0, nb, 0) + jnp.where(n < max_exact, n, large)


def _bucket_saturation_distance():
    nb = NUM_BUCKETS // 2
    max_exact = nb // 2
    n = np.arange(1, 4 * MAX_DISTANCE, dtype=np.float64)
    large = max_exact + np.floor(np.log(n / max_exact) / math.log(MAX_DISTANCE / max_exact) * (nb - max_exact))
    unsat = np.nonzero(np.minimum(large, nb - 1) < nb - 1)[0]
    return int(n[unsat[-1]]) + 2


def _bias_tile_kernel(t, rows, rb_ref, o_ref):
    h = pl.program_id(0)
    d = pl.program_id(1)
    offset = (d - 1) * t

    def body(c, carry):
        r0 = pl.multiple_of(c * rows, rows)
        key = lax.broadcasted_iota(jnp.int32, (rows, t), 0) + r0
        query = lax.broadcasted_iota(jnp.int32, (rows, t), 1)
        bucket = _t5_bucket(key - query + offset)
        bias = jnp.zeros((rows, t), jnp.float32)
        for b in range(NUM_BUCKETS):
            bias = jnp.where(bucket == b, rb_ref[b, h] * LOG2E, bias)
        o_ref[0, 0, pl.ds(r0, rows), :] = bias
        return carry

    lax.fori_loop(0, t // rows, body, 0)


def bias_tiles(rel_bias, t):
    n_heads = rel_bias.shape[1]
    rows = min(128, t)
    return pl.pallas_call(
        functools.partial(_bias_tile_kernel, t, rows),
        grid=(n_heads, 3),
        in_specs=[pl.BlockSpec(memory_space=pltpu.SMEM)],
        out_specs=pl.BlockSpec((1, 1, t, t), lambda h, d: (h, d, 0, 0)),
        out_shape=jax.ShapeDtypeStruct((n_heads, 3, t, t), jnp.float32),
        compiler_params=_cparams(("arbitrary", "arbitrary")),
        name="t5_bias_tiles",
    )(rel_bias)


def _diff_attn_kernel(lambda_init, qt_ref, k_ref, vt_ref, b_ref, rb_ref, lam_ref, sg_ref, o_ref,
                      m_sc, l_sc, acc_sc):
    h = pl.program_id(0)
    qi = pl.program_id(1)
    ki = pl.program_id(2)
    last = pl.num_programs(2) - 1
    nb = NUM_BUCKETS // 2

    @pl.when(ki == 0)
    def _():
        m_sc[...] = jnp.full(m_sc.shape, NEG_BIG, jnp.float32)
        l_sc[...] = jnp.zeros(l_sc.shape, jnp.float32)
        acc_sc[...] = jnp.zeros(acc_sc.shape, jnp.float32)

    def update(bias_tile, bias_const):
        tk = k_ref.shape[0]
        kc = tk // KEY_SPLIT
        stages = [(mp, c) for mp in range(2) for c in range(KEY_SPLIT)]

        def scores(stage):
            mp, c = stage
            rows = slice(mp * DIFF_HEAD_DIM, (mp + 1) * DIFF_HEAD_DIM)
            return jnp.dot(k_ref[c * kc:(c + 1) * kc, rows], qt_ref[rows, :],
                           preferred_element_type=jnp.float32)

        pending = [scores(stage) for stage in stages[:SCORE_LOOKAHEAD]]
        for n, (mp, c) in enumerate(stages):
            if n + SCORE_LOOKAHEAD < len(stages):
                pending.append(scores(stages[n + SCORE_LOOKAHEAD]))
            st = pending.pop(0)
            m_prev = m_sc[mp]
            if bias_tile:
                st = st + b_ref[0, 0, c * kc:(c + 1) * kc, :]
                m_new = jnp.maximum(m_prev, jnp.max(st, axis=0, keepdims=True))
                shift = m_new
            else:
                m_new = jnp.maximum(m_prev, jnp.max(st, axis=0, keepdims=True) + bias_const)
                shift = m_new - bias_const
            alpha = jnp.exp2(m_prev - m_new)
            p = jnp.exp2(st - shift)
            l_sc[mp] = alpha * l_sc[mp] + jnp.sum(p, axis=0, keepdims=True)
            vt = vt_ref[:, c * kc:(c + 1) * kc]
            acc_sc[mp] = alpha * acc_sc[mp] + jnp.dot(vt, p.astype(vt.dtype), preferred_element_type=jnp.float32)
            m_sc[mp] = m_new

    near = jnp.abs(ki - qi) <= 1

    @pl.when(near)
    def _():
        update(True, None)

    @pl.when(jnp.logical_not(near))
    def _():
        update(False, jnp.where(ki < qi, rb_ref[nb - 1, h], rb_ref[NUM_BUCKETS - 1, h]) * LOG2E)

    @pl.when(ki == last)
    def _():
        lv = lam_ref[...]
        lam = (jnp.exp(jnp.sum(lv[0:1] * lv[1:2], axis=1, keepdims=True))
               - jnp.exp(jnp.sum(lv[2:3] * lv[3:4], axis=1, keepdims=True)) + lambda_init)
        o = (acc_sc[0] / l_sc[0] - lam * (acc_sc[1] / l_sc[1])).T
        y = o * lax.rsqrt(jnp.mean(o * o, axis=-1, keepdims=True) + NORM_EPS)
        o_ref[...] = ((y * sg_ref[...]) * (1.0 - lambda_init)).astype(o_ref.dtype)


def diff_attention(qvt, k, btiles, rel_bias, lam_vecs, sub_gain, lambda_init, n_heads):
    s = k.shape[0]
    t = btiles.shape[-1]
    assert t + 1 >= _bucket_saturation_distance()
    w = DIFF_V_DIM
    return pl.pallas_call(
        functools.partial(_diff_attn_kernel, lambda_init),
        grid=(n_heads, s // t, s // t),
        in_specs=[pl.BlockSpec((w, t), lambda h, i, j: (h, i)),
                  pl.BlockSpec((t, w), lambda h, i, j: (j, h)),
                  pl.BlockSpec((w, t), lambda h, i, j: (n_heads + h, j)),
                  pl.BlockSpec((1, 1, t, t), lambda h, i, j: (h, jnp.clip(j - i + 1, 0, 2), 0, 0)),
                  pl.BlockSpec(memory_space=pltpu.SMEM),
                  pl.BlockSpec((4, DIFF_HEAD_DIM), lambda h, i, j: (0, 0)),
                  pl.BlockSpec((1, w), lambda h, i, j: (0, 0))],
        out_specs=pl.BlockSpec((t, w), lambda h, i, j: (i, h)),
        out_shape=jax.ShapeDtypeStruct((s, n_heads * w), jnp.bfloat16),
        scratch_shapes=[pltpu.VMEM((2, 1, t), jnp.float32),
                        pltpu.VMEM((2, 1, t), jnp.float32),
                        pltpu.VMEM((2, w, t), jnp.float32)],
        compiler_params=_cparams(("arbitrary", "arbitrary", "arbitrary")),
        name="diff_attention",
    )(qvt, k, qvt, btiles, rel_bias, lam_vecs, sub_gain.reshape(1, w))


def _route_kernel(x_ref, g_ref, rw_ref, h_ref, aff_ref):
    x = x_ref[...]
    y = x * lax.rsqrt(jnp.mean(x * x, axis=-1, keepdims=True) + NORM_EPS)
    hn = y * g_ref[...]
    h_ref[...] = hn
    logits = lax.dot_general(rw_ref[...], hn.astype(jnp.bfloat16), (((1,), (1,)), ((), ())),
                             preferred_element_type=jnp.float32)
    e = jnp.exp(logits - jnp.max(logits, axis=0, keepdims=True))
    aff_ref[...] = e / jnp.sum(e, axis=0, keepdims=True)


def route(x, g, router_w_t):
    s, d = x.shape
    n_e = router_w_t.shape[0]
    ts = min(ROUTE_ROWS, s)
    return pl.pallas_call(
        _route_kernel,
        grid=(s // ts,),
        in_specs=[pl.BlockSpec((ts, d), lambda i: (i, 0)),
                  pl.BlockSpec((1, d), lambda i: (0, 0)),
                  pl.BlockSpec((n_e, d), lambda i: (0, 0))],
        out_specs=[pl.BlockSpec((ts, d), lambda i: (i, 0)),
                   pl.BlockSpec((n_e, ts), lambda i: (0, i))],
        out_shape=[jax.ShapeDtypeStruct((s, d), jnp.float32),
                   jax.ShapeDtypeStruct((n_e, s), jnp.float32)],
        compiler_params=_cparams(("arbitrary",)),
        name="moe_route",
    )(x, g.reshape(1, d), router_w_t)


def _split3_bf16(x):
    hi = x.astype(jnp.bfloat16)
    r1 = x - hi.astype(jnp.float32)
    mid = r1.astype(jnp.bfloat16)
    lo = (r1 - mid.astype(jnp.float32)).astype(jnp.bfloat16)
    return hi, mid, lo


def _select_kernel(cap, aff_ref, idx_ref, gate_ref):
    a = aff_ref[0]
    nch = a.shape[0]
    bits = pltpu.bitcast(a, jnp.int32)

    def count(mask):
        c = jnp.sum(mask.astype(jnp.float32), axis=1, keepdims=True)
        return jnp.sum(c, axis=0, keepdims=True)

    thr = jnp.zeros((1, 1), jnp.int32)
    for b in range(30, -1, -1):
        cand = thr | (1 << b)
        thr = jnp.where(count(bits >= cand) >= cap, cand, thr)

    li = lax.broadcasted_iota(jnp.int32, (LANES, LANES), 0)
    lj = lax.broadcasted_iota(jnp.int32, (LANES, LANES), 1)
    tri_incl = (li <= lj).astype(jnp.bfloat16)
    ci = lax.broadcasted_iota(jnp.int32, (nch, nch), 0)
    cj = lax.broadcasted_iota(jnp.int32, (nch, nch), 1)
    before_rows = (cj < ci).astype(jnp.bfloat16)
    before_cols = (ci < cj).astype(jnp.bfloat16)
    ones8 = jnp.ones((8, LANES), jnp.bfloat16)

    def prefix(mask_f32):
        mb = mask_f32.astype(jnp.bfloat16)
        incl = jnp.dot(mb, tri_incl, preferred_element_type=jnp.float32)
        start_col = jnp.sum(jnp.dot(before_rows, mb, preferred_element_type=jnp.float32),
                            axis=1, keepdims=True)
        tot = lax.dot_general(ones8, mb, (((1,), (1,)), ((), ())), preferred_element_type=jnp.float32)
        start_lane = jnp.dot(tot.astype(jnp.bfloat16), before_cols,
                             preferred_element_type=jnp.float32)[0:1]
        return incl, start_col, start_lane

    gt = bits > thr
    eq = bits == thr
    need = cap - count(gt)
    eq_f = eq.astype(jnp.float32)
    incl_eq, start_eq, _ = prefix(eq_f)
    keep = jnp.logical_and(eq, (incl_eq - eq_f + start_eq) < need)
    sel_f = jnp.logical_or(gt, keep).astype(jnp.float32)
    incl, _, start_lane = prefix(sel_f)

    slot = lax.broadcasted_iota(jnp.int32, (cap, 1), 0).astype(jnp.float32)
    chunk = jnp.sum((start_lane <= slot).astype(jnp.float32), axis=1, keepdims=True) - 1.0
    chunk_iota = lax.broadcasted_iota(jnp.int32, (cap, nch), 1).astype(jnp.float32)
    onehot = chunk_iota == chunk
    chunk_start = jnp.sum(jnp.where(onehot, start_lane, 0.0), axis=1, keepdims=True)
    onehot_b = onehot.astype(jnp.bfloat16)
    rank = slot - chunk_start
    incl_rows = jnp.dot(onehot_b, incl.astype(jnp.bfloat16), preferred_element_type=jnp.float32)
    lane = jnp.sum((incl_rows <= rank).astype(jnp.float32), axis=1, keepdims=True)
    idx_ref[0] = (chunk * LANES + lane).astype(jnp.int32)

    hi, mid, lo = _split3_bf16(a)
    arow = (jnp.dot(onehot_b, hi, preferred_element_type=jnp.float32)
            + jnp.dot(onehot_b, mid, preferred_element_type=jnp.float32)
            + jnp.dot(onehot_b, lo, preferred_element_type=jnp.float32))
    lane_iota = lax.broadcasted_iota(jnp.int32, (cap, LANES), 1).astype(jnp.float32)
    gate_ref[0] = jnp.sum(jnp.where(lane_iota == lane, arow, 0.0), axis=1, keepdims=True)


def select(aff_t, cap):
    n_e, s = aff_t.shape
    nch = s // LANES
    return pl.pallas_call(
        functools.partial(_select_kernel, cap),
        grid=(n_e,),
        in_specs=[pl.BlockSpec((1, nch, LANES), lambda e: (e, 0, 0))],
        out_specs=[pl.BlockSpec((1, cap, 1), lambda e: (e, 0, 0)),
                   pl.BlockSpec((1, cap, 1), lambda e: (e, 0, 0))],
        out_shape=[jax.ShapeDtypeStruct((n_e, cap, 1), jnp.int32),
                   jax.ShapeDtypeStruct((n_e, cap, 1), jnp.float32)],
        compiler_params=_cparams(("arbitrary",)),
        name="moe_select",
    )(aff_t.reshape(n_e, nch, LANES))


def _row_copies(src_hbm, dst_vmem, sem, idx_ref, base, n_rows, to_vmem):
    def copy(r):
        tok = idx_ref[base + r]
        hbm_row = src_hbm.at[pl.ds(tok, 1)]
        vmem_row = dst_vmem.at[pl.ds(r, 1)]
        if to_vmem:
            return pltpu.make_async_copy(hbm_row, vmem_row, sem)
        return pltpu.make_async_copy(vmem_row, hbm_row, sem)

    def start(r, c):
        copy(r).start()
        return c

    def wait(r, c):
        copy(r).wait()
        return c

    lax.fori_loop(0, n_rows, start, 0)
    lax.fori_loop(0, n_rows, wait, 0)


def _expert_kernel(n_k, blocks_per_e, idx_ref, h_hbm, gate_ref, wg_ref, wu_ref, wd_ref, y_ref,
                   xs, g_acc, u_acc, act, sem):
    blk = pl.program_id(0)
    j = pl.program_id(1)
    rows = xs.shape[0]
    kc = wg_ref.shape[1]

    @pl.when(j == 0)
    def _():
        _row_copies(h_hbm, xs, sem, idx_ref, blk * rows, rows, True)
        g_acc[...] = jnp.zeros(g_acc.shape, jnp.float32)
        u_acc[...] = jnp.zeros(u_acc.shape, jnp.float32)

    for jj in range(n_k):
        @pl.when(j == jj)
        def _():
            xk = xs[:, jj * kc:(jj + 1) * kc].astype(jnp.bfloat16)
            g_acc[...] += jnp.dot(xk, wg_ref[0].astype(jnp.bfloat16), preferred_element_type=jnp.float32)
            u_acc[...] += jnp.dot(xk, wu_ref[0].astype(jnp.bfloat16), preferred_element_type=jnp.float32)

    @pl.when(j == n_k - 1)
    def _():
        g = g_acc[...]
        act[...] = ((g * (1.0 / (1.0 + jnp.exp(-g)))) * u_acc[...]).astype(act.dtype)

    @pl.when(j >= n_k)
    def _():
        y = jnp.dot(act[...], wd_ref[0].astype(jnp.bfloat16), preferred_element_type=jnp.float32)
        y_ref[...] = y * gate_ref[...]


def expert_ffn(h, idx_flat, gate_col, w_gate, w_up, w_down):
    s, d = h.shape
    n_e, _, ff = w_gate.shape
    n_slots = idx_flat.shape[0]
    cap = n_slots // n_e
    rows = min(FFN_ROWS, cap)
    blocks_per_e = cap // rows
    kc = min(FFN_KCHUNK, d)
    n_k = d // kc
    n_n = d // kc
    grid_spec = pltpu.PrefetchScalarGridSpec(
        num_scalar_prefetch=1,
        grid=(n_slots // rows, n_k + n_n),
        in_specs=[pl.BlockSpec(memory_space=pl.ANY),
                  pl.BlockSpec((rows, 1), lambda b, j, idx: (b, 0)),
                  pl.BlockSpec((1, kc, ff), lambda b, j, idx: (b // blocks_per_e, jnp.minimum(j, n_k - 1), 0)),
                  pl.BlockSpec((1, kc, ff), lambda b, j, idx: (b // blocks_per_e, jnp.minimum(j, n_k - 1), 0)),
                  pl.BlockSpec((1, ff, kc), lambda b, j, idx: (b // blocks_per_e, 0, jnp.maximum(j - n_k, 0)))],
        out_specs=pl.BlockSpec((rows, kc), lambda b, j, idx: (b, jnp.maximum(j - n_k, 0))),
        scratch_shapes=[pltpu.VMEM((rows, d), jnp.float32),
                        pltpu.VMEM((rows, ff), jnp.float32),
                        pltpu.VMEM((rows, ff), jnp.float32),
                        pltpu.VMEM((rows, ff), jnp.bfloat16),
                        pltpu.SemaphoreType.DMA(())],
    )
    return pl.pallas_call(
        functools.partial(_expert_kernel, n_k, blocks_per_e),
        grid_spec=grid_spec,
        out_shape=jax.ShapeDtypeStruct((n_slots, d), jnp.float32),
        compiler_params=_cparams(("arbitrary", "arbitrary")),
        name="moe_expert_ffn",
    )(idx_flat, h, gate_col, w_gate, w_up, w_down)


def _combine_kernel(idx_ref, x_in_hbm, y_ref, x_hbm, stage, sem_in, sem_out):
    del x_in_hbm
    blk = pl.program_id(0)
    rows = stage.shape[0]
    _row_copies(x_hbm, stage, sem_in, idx_ref, blk * rows, rows, True)
    stage[...] += y_ref[...]
    _row_copies(x_hbm, stage, sem_out, idx_ref, blk * rows, rows, False)


def combine(x, y_slots, idx_flat, n_e):
    s, d = x.shape
    n_slots = idx_flat.shape[0]
    cap = n_slots // n_e
    rows = min(COMBINE_ROWS, cap)
    grid_spec = pltpu.PrefetchScalarGridSpec(
        num_scalar_prefetch=1,
        grid=(n_slots // rows,),
        in_specs=[pl.BlockSpec(memory_space=pl.ANY),
                  pl.BlockSpec((rows, d), lambda b, idx: (b, 0))],
        out_specs=pl.BlockSpec(memory_space=pl.ANY),
        scratch_shapes=[pltpu.VMEM((rows, d), jnp.float32),
                        pltpu.SemaphoreType.DMA(()),
                        pltpu.SemaphoreType.DMA(())],
    )
    return pl.pallas_call(
        _combine_kernel,
        grid_spec=grid_spec,
        out_shape=jax.ShapeDtypeStruct((s, d), jnp.float32),
        input_output_aliases={1: 0},
        compiler_params=_cparams(("arbitrary",)),
        name="moe_combine",
    )(idx_flat, x, y_slots)


def expert_choice_ffn(x, norm_g, router_w, w_gate, w_up, w_down):
    s, d = x.shape
    n_e = w_gate.shape[0]
    cap = EC_CAPACITY_FACTOR * s // n_e
    bf = jnp.bfloat16
    h, aff_t = route(x, norm_g, router_w.T.astype(bf))
    idx, gate = select(aff_t, cap)
    idx_flat = idx.reshape(n_e * cap)
    y = expert_ffn(h, idx_flat, gate.reshape(n_e * cap, 1), w_gate, w_up, w_down)
    return combine(x, y, idx_flat, n_e)


def kernel(x, attn_norm, ffn_norm, final_norm, gqa_w_in, gqa_q_norm, gqa_k_norm, gqa_w_out,
           diff_w_in, diff_lambda_q1, diff_lambda_k1, diff_lambda_q2, diff_lambda_k2,
           diff_sub_norm, diff_w_out, rel_bias, router_w, expert_w_gate, expert_w_up,
           expert_w_down):
    b, s, d = x.shape
    assert b == 1
    bf = jnp.bfloat16
    depth = attn_norm.shape[0]
    n_q = d // HEAD_DIM
    n_kv = n_q // GQA_GROUP
    n_diff = d // DIFF_V_DIM
    diff_t = min(DIFF_T, s)
    btiles = bias_tiles(rel_bias, diff_t) if depth > 1 else None

    xs = x.reshape(s, d)
    rope_tables = _rope_tables(s)
    for i in range(depth):
        j = i // N_MIXERS
        if i % N_MIXERS == 0:
            h = rmsnorm(xs, attn_norm[i], bf)
            proj = matmul(h, gqa_w_in[j], jnp.float32)
            qt = head_prep(proj, 0, n_q, rope_tables, gqa_q_norm[j], scale=HEAD_DIM ** -0.5 * LOG2E, transpose=True)
            k = head_prep(proj, n_q, n_kv, rope_tables, gqa_k_norm[j])
            vt = head_prep(proj, n_q + n_kv, n_kv, transpose=True)
            o = gqa_attention(qt, k, vt)
            xs = matmul(o, gqa_w_out[j], jnp.float32, residual=xs)
        else:
            lambda_init = 0.8 - 0.6 * math.exp(-0.3 * i)
            dqk = n_diff * 2 * DIFF_HEAD_DIM
            h, ht = rmsnorm(xs, attn_norm[i], bf, with_transpose=True)
            w_in = diff_w_in[j]
            w_qv_t = jnp.concatenate([w_in[:, :dqk].T, w_in[:, 2 * dqk:].T], axis=0).astype(bf)
            qvt = matmul(w_qv_t, ht, bf, row_scale=(dqk, DIFF_HEAD_DIM ** -0.5 * LOG2E))
            k = matmul(h, w_in, bf, w_col_offset=dqk, n=dqk)
            lam_vecs = jnp.stack([diff_lambda_q1[j], diff_lambda_k1[j], diff_lambda_q2[j], diff_lambda_k2[j]])
            o = diff_attention(qvt, k, btiles, rel_bias, lam_vecs, diff_sub_norm[j], lambda_init, n_diff)
            xs = matmul(o, diff_w_out[j], jnp.float32, residual=xs)
        xs = expert_choice_ffn(xs, ffn_norm[i], router_w[i], expert_w_gate[i], expert_w_up[i], expert_w_down[i])
    return rmsnorm(xs, final_norm, jnp.float32).reshape(b, s, d)
```

```python
import functools
import math

import numpy as np
import jax
import jax.numpy as jnp
from jax import lax
from jax.experimental import pallas as pl
from jax.experimental.pallas import tpu as pltpu

HEAD_DIM = 128
GQA_GROUP = 4
DIFF_HEAD_DIM = 128
DIFF_V_DIM = 2 * DIFF_HEAD_DIM
EC_CAPACITY_FACTOR = 2
NUM_BUCKETS = 32
MAX_DISTANCE = 128
GRID_W = 64
ROPE_THETA = 10000.0
NORM_EPS = 1e-6
N_MIXERS = 2

LANES = 128
VMEM_LIMIT_BYTES = 56 * 1024 * 1024

NORM_ROWS = 256
MM_TM, MM_TN = 1024, 512
PREP_ROWS = 1024
GQA_TQ, GQA_TK = 512, 4096
DIFF_T = 1024
STAGE_KEYS = 1024
SCORE_LOOKAHEAD = 1
ROUTE_ROWS = 512
FFN_ROWS = 1024
FFN_KCHUNK = 512
COMBINE_ROWS = 512
NEG_BIG = -1e30
LOG2E = 1.4426950408889634


def _cparams(semantics):
    return pltpu.CompilerParams(dimension_semantics=semantics, vmem_limit_bytes=VMEM_LIMIT_BYTES)


def _rmsnorm_kernel(x_ref, g_ref, o_ref):
    x = x_ref[...]
    y = x * lax.rsqrt(jnp.mean(x * x, axis=-1, keepdims=True) + NORM_EPS)
    o_ref[...] = (y * g_ref[...]).astype(o_ref.dtype)


def rmsnorm(x, g, out_dtype):
    s, d = x.shape
    return pl.pallas_call(
        _rmsnorm_kernel,
        grid=(s // NORM_ROWS,),
        in_specs=[pl.BlockSpec((NORM_ROWS, d), lambda i: (i, 0)),
                  pl.BlockSpec((1, d), lambda i: (0, 0))],
        out_specs=pl.BlockSpec((NORM_ROWS, d), lambda i: (i, 0)),
        out_shape=jax.ShapeDtypeStruct((s, d), out_dtype),
        compiler_params=_cparams(("arbitrary",)),
        name="rmsnorm",
    )(x, g.reshape(1, d))


def _mm_kernel(col_scale, has_residual, a_ref, w_ref, *rest):
    o_ref = rest[-1]
    acc = jnp.dot(a_ref[...], w_ref[...].astype(jnp.bfloat16), preferred_element_type=jnp.float32)
    if col_scale is not None:
        n_tiles, factor = col_scale
        acc = acc * jnp.where(pl.program_id(1) < n_tiles, factor, 1.0).astype(jnp.float32)
    if has_residual:
        acc = rest[0][...] + acc
    o_ref[...] = acc.astype(o_ref.dtype)


def matmul(a, w_stack, layer, out_dtype, residual=None, col_scale=None):
    m, k = a.shape
    n = w_stack.shape[2]
    tm, tn = min(MM_TM, m), min(MM_TN, n)
    in_specs = [pl.BlockSpec((tm, k), lambda i, j: (i, 0)),
                pl.BlockSpec((None, k, tn), lambda i, j: (layer, 0, j))]
    args = [a, w_stack]
    if residual is not None:
        in_specs.append(pl.BlockSpec((tm, tn), lambda i, j: (i, j)))
        args.append(residual)
    if col_scale is not None:
        assert col_scale[0] % tn == 0
        col_scale = (col_scale[0] // tn, col_scale[1])
    return pl.pallas_call(
        functools.partial(_mm_kernel, col_scale, residual is not None),
        grid=(m // tm, n // tn),
        in_specs=in_specs,
        out_specs=pl.BlockSpec((tm, tn), lambda i, j: (i, j)),
        out_shape=jax.ShapeDtypeStruct((m, n), out_dtype),
        compiler_params=_cparams(("arbitrary", "arbitrary")),
        name="proj_matmul",
    )(*args)


def _rope_tables(s):
    half = HEAD_DIM // 2
    rows = s // GRID_W
    row = jnp.repeat(jnp.arange(rows, dtype=jnp.int32), GRID_W)
    col = jnp.tile(jnp.arange(GRID_W, dtype=jnp.int32), rows)
    freq = ROPE_THETA ** (-jnp.arange(0, half, 2, dtype=jnp.float32) / half)
    ang_r = row.astype(jnp.float32)[:, None] * freq[None, :]
    ang_c = col.astype(jnp.float32)[:, None] * freq[None, :]
    cos_t = jnp.concatenate([jnp.cos(ang_r), jnp.cos(ang_r), jnp.cos(ang_c), jnp.cos(ang_c)], axis=-1)
    sin_t = jnp.concatenate([-jnp.sin(ang_r), jnp.sin(ang_r), -jnp.sin(ang_c), jnp.sin(ang_c)], axis=-1)
    return cos_t, sin_t


def _head_prep_kernel(rope, transpose, scale, p_ref, *refs):
    o_ref = refs[-1]
    r = p_ref[...].astype(jnp.float32)
    if rope:
        cos_ref, sin_ref, g_ref = refs[:3]
        y = r * lax.rsqrt(jnp.mean(r * r, axis=-1, keepdims=True) + NORM_EPS)
        y = y * g_ref[...]
        quarter = HEAD_DIM // 4
        lane = lax.broadcasted_iota(jnp.int32, y.shape, 1)
        first = (lane & (2 * quarter - 1)) < quarter
        partner = jnp.where(first,
                            pltpu.roll(y, HEAD_DIM - quarter, axis=1),
                            pltpu.roll(y, quarter, axis=1))
        r = y * cos_ref[...] + partner * sin_ref[...]
    if scale is not None:
        r = r * scale
    if transpose:
        o_ref[0] = r.T.astype(o_ref.dtype)
    else:
        o_ref[...] = r.astype(o_ref.dtype)


def head_prep(proj, first_head, n_heads, tables=None, gain=None, scale=None, transpose=False):
    s = proj.shape[0]
    ts = min(PREP_ROWS, s)
    rope = tables is not None
    in_specs = [pl.BlockSpec((ts, HEAD_DIM), lambda i, h: (i, first_head + h))]
    args = [proj]
    if rope:
        in_specs += [pl.BlockSpec((ts, HEAD_DIM), lambda i, h: (i, 0)),
                     pl.BlockSpec((ts, HEAD_DIM), lambda i, h: (i, 0)),
                     pl.BlockSpec((1, HEAD_DIM), lambda i, h: (0, 0))]
        args += [tables[0], tables[1], gain.reshape(1, HEAD_DIM)]
    if transpose:
        out_spec = pl.BlockSpec((1, HEAD_DIM, ts), lambda i, h: (h, 0, i))
        out_shape = jax.ShapeDtypeStruct((n_heads, HEAD_DIM, s), jnp.bfloat16)
    else:
        out_spec = pl.BlockSpec((ts, HEAD_DIM), lambda i, h: (i, h))
        out_shape = jax.ShapeDtypeStruct((s, n_heads * HEAD_DIM), jnp.bfloat16)
    return pl.pallas_call(
        functools.partial(_head_prep_kernel, rope, transpose, scale),
        grid=(s // ts, n_heads),
        in_specs=in_specs,
        out_specs=out_spec,
        out_shape=out_shape,
        compiler_params=_cparams(("arbitrary", "arbitrary")),
        name="head_prep",
    )(*args)


def _gqa_attn_kernel(qt_ref, k_ref, vt_ref, o_ref, m_sc, l_sc, acc_sc):
    ki = pl.program_id(2)
    last = pl.num_programs(2) - 1

    @pl.when(ki == 0)
    def _():
        m_sc[...] = jnp.full(m_sc.shape, NEG_BIG, jnp.float32)
        l_sc[...] = jnp.zeros(l_sc.shape, jnp.float32)
        acc_sc[...] = jnp.zeros(acc_sc.shape, jnp.float32)

    tk = k_ref.shape[0]
    kc = min(STAGE_KEYS, tk)
    stages = [(g, c) for g in range(GQA_GROUP) for c in range(tk // kc)]

    def scores(stage):
        g, c = stage
        return jnp.dot(k_ref[c * kc:(c + 1) * kc, :], qt_ref[g], preferred_element_type=jnp.float32)

    pending = [scores(stage) for stage in stages[:SCORE_LOOKAHEAD]]
    for n, (g, c) in enumerate(stages):
        if n + SCORE_LOOKAHEAD < len(stages):
            pending.append(scores(stages[n + SCORE_LOOKAHEAD]))
        st = pending.pop(0)
        m_prev = m_sc[g]
        m_new = jnp.maximum(m_prev, jnp.max(st, axis=0, keepdims=True))
        alpha = jnp.exp2(m_prev - m_new)
        p = jnp.exp2(st - m_new)
        l_sc[g] = alpha * l_sc[g] + jnp.sum(p, axis=0, keepdims=True)
        vt = vt_ref[0, :, c * kc:(c + 1) * kc]
        acc_sc[g] = alpha * acc_sc[g] + jnp.dot(vt, p.astype(vt.dtype), preferred_element_type=jnp.float32)
        m_sc[g] = m_new

    @pl.when(ki == last)
    def _():
        for g in range(GQA_GROUP):
            o_ref[:, g * HEAD_DIM:(g + 1) * HEAD_DIM] = (acc_sc[g] / l_sc[g]).T.astype(o_ref.dtype)


def gqa_attention(qt, k, vt):
    n_q, _, s = qt.shape
    n_kv = vt.shape[0]
    tq, tk = min(GQA_TQ, s), min(GQA_TK, s)
    gw = GQA_GROUP * HEAD_DIM
    return pl.pallas_call(
        _gqa_attn_kernel,
        grid=(n_kv, s // tq, s // tk),
        in_specs=[pl.BlockSpec((GQA_GROUP, HEAD_DIM, tq), lambda h, i, j: (h, 0, i)),
                  pl.BlockSpec((tk, HEAD_DIM), lambda h, i, j: (j, h)),
                  pl.BlockSpec((1, HEAD_DIM, tk), lambda h, i, j: (h, 0, j))],
        out_specs=pl.BlockSpec((tq, gw), lambda h, i, j: (i, h)),
        out_shape=jax.ShapeDtypeStruct((s, n_q * HEAD_DIM), jnp.bfloat16),
        scratch_shapes=[pltpu.VMEM((GQA_GROUP, 1, tq), jnp.float32),
                        pltpu.VMEM((GQA_GROUP, 1, tq), jnp.float32),
                        pltpu.VMEM((GQA_GROUP, HEAD_DIM, tq), jnp.float32)],
        compiler_params=_cparams(("arbitrary", "arbitrary", "arbitrary")),
        name="gqa_attention",
    )(qt, k, vt)


def _t5_bucket(rel):
    nb = NUM_BUCKETS // 2
    max_exact = nb // 2
    n = jnp.abs(rel)
    large = max_exact + (jnp.log(jnp.maximum(n, 1).astype(jnp.float32) / max_exact)
                         / math.log(MAX_DISTANCE / max_exact) * (nb - max_exact)).astype(jnp.int32)
    large = jnp.minimum(large, nb - 1)
    return jnp.where(rel > 0, nb, 0) + jnp.where(n < max_exact, n, large)


def _bucket_saturation_distance():
    nb = NUM_BUCKETS // 2
    max_exact = nb // 2
    n = np.arange(1, 4 * MAX_DISTANCE, dtype=np.float64)
    large = max_exact + np.floor(np.log(n / max_exact) / math.log(MAX_DISTANCE / max_exact) * (nb - max_exact))
    unsat = np.nonzero(np.minimum(large, nb - 1) < nb - 1)[0]
    return int(n[unsat[-1]]) + 2


def _bias_tile_kernel(t, rows, rb_ref, o_ref):
    h = pl.program_id(0)
    d = pl.program_id(1)
    offset = (d - 1) * t

    def body(c, carry):
        r0 = pl.multiple_of(c * rows, rows)
        key = lax.broadcasted_iota(jnp.int32, (rows, t), 0) + r0
        query = lax.broadcasted_iota(jnp.int32, (rows, t), 1)
        bucket = _t5_bucket(key - query + offset)
        bias = jnp.zeros((rows, t), jnp.float32)
        for b in range(NUM_BUCKETS):
            bias = jnp.where(bucket == b, rb_ref[b, h] * LOG2E, bias)
        o_ref[0, 0, pl.ds(r0, rows), :] = bias
        return carry

    lax.fori_loop(0, t // rows, body, 0)


def bias_tiles(rel_bias, t):
    n_heads = rel_bias.shape[1]
    rows = min(128, t)
    return pl.pallas_call(
        functools.partial(_bias_tile_kernel, t, rows),
        grid=(n_heads, 3),
        in_specs=[pl.BlockSpec(memory_space=pltpu.SMEM)],
        out_specs=pl.BlockSpec((1, 1, t, t), lambda h, d: (h, d, 0, 0)),
        out_shape=jax.ShapeDtypeStruct((n_heads, 3, t, t), jnp.float32),
        compiler_params=_cparams(("arbitrary", "arbitrary")),
        name="t5_bias_tiles",
    )(rel_bias)


def _diff_attn_kernel(lambda_init, qt_ref, k_ref, vt_ref, b_ref, rb_ref, lam_ref, sg_ref, o_ref,
                      m_sc, l_sc, acc_sc):
    h = pl.program_id(0)
    qi = pl.program_id(1)
    ki = pl.program_id(2)
    last = pl.num_programs(2) - 1
    nb = NUM_BUCKETS // 2

    @pl.when(ki == 0)
    def _():
        m_sc[...] = jnp.full(m_sc.shape, NEG_BIG, jnp.float32)
        l_sc[...] = jnp.zeros(l_sc.shape, jnp.float32)
        acc_sc[...] = jnp.zeros(acc_sc.shape, jnp.float32)

    def update(bias_tile, bias_const):
        tk = k_ref.shape[0]
        kc = min(STAGE_KEYS // 2, tk)
        stages = [(mp, c) for mp in range(2) for c in range(tk // kc)]

        def scores(stage):
            mp, c = stage
            rows = slice(mp * DIFF_HEAD_DIM, (mp + 1) * DIFF_HEAD_DIM)
            return jnp.dot(k_ref[c * kc:(c + 1) * kc, rows], qt_ref[0, rows, :],
                           preferred_element_type=jnp.float32)

        pending = [scores(stage) for stage in stages[:SCORE_LOOKAHEAD]]
        for n, (mp, c) in enumerate(stages):
            if n + SCORE_LOOKAHEAD < len(stages):
                pending.append(scores(stages[n + SCORE_LOOKAHEAD]))
            st = pending.pop(0)
            m_prev = m_sc[mp]
            if bias_tile:
                st = st + b_ref[0, 0, c * kc:(c + 1) * kc, :]
                m_new = jnp.maximum(m_prev, jnp.max(st, axis=0, keepdims=True))
                shift = m_new
            else:
                m_new = jnp.maximum(m_prev, jnp.max(st, axis=0, keepdims=True) + bias_const)
                shift = m_new - bias_const
            alpha = jnp.exp2(m_prev - m_new)
            p = jnp.exp2(st - shift)
            l_sc[mp] = alpha * l_sc[mp] + jnp.sum(p, axis=0, keepdims=True)
            vt = vt_ref[0, :, c * kc:(c + 1) * kc]
            acc_sc[mp] = alpha * acc_sc[mp] + jnp.dot(vt, p.astype(vt.dtype), preferred_element_type=jnp.float32)
            m_sc[mp] = m_new

    near = jnp.abs(ki - qi) <= 1

    @pl.when(near)
    def _():
        update(True, None)

    @pl.when(jnp.logical_not(near))
    def _():
        update(False, jnp.where(ki < qi, rb_ref[nb - 1, h], rb_ref[NUM_BUCKETS - 1, h]) * LOG2E)

    @pl.when(ki == last)
    def _():
        lv = lam_ref[...]
        lam = (jnp.exp(jnp.sum(lv[0:1] * lv[1:2], axis=1, keepdims=True))
               - jnp.exp(jnp.sum(lv[2:3] * lv[3:4], axis=1, keepdims=True)) + lambda_init)
        o = (acc_sc[0] / l_sc[0] - lam * (acc_sc[1] / l_sc[1])).T
        y = o * lax.rsqrt(jnp.mean(o * o, axis=-1, keepdims=True) + NORM_EPS)
        o_ref[...] = ((y * sg_ref[...]) * (1.0 - lambda_init)).astype(o_ref.dtype)


def diff_attention(qt, proj, vt, btiles, rel_bias, lam_vecs, sub_gain, lambda_init):
    n_heads, w, s = qt.shape
    t = btiles.shape[-1]
    assert t + 1 >= _bucket_saturation_distance()
    return pl.pallas_call(
        functools.partial(_diff_attn_kernel, lambda_init),
        grid=(n_heads, s // t, s // t),
        in_specs=[pl.BlockSpec((1, w, t), lambda h, i, j: (h, 0, i)),
                  pl.BlockSpec((t, w), lambda h, i, j: (j, n_heads + h)),
                  pl.BlockSpec((1, w, t), lambda h, i, j: (h, 0, j)),
                  pl.BlockSpec((1, 1, t, t), lambda h, i, j: (h, jnp.clip(j - i + 1, 0, 2), 0, 0)),
                  pl.BlockSpec(memory_space=pltpu.SMEM),
                  pl.BlockSpec((4, DIFF_HEAD_DIM), lambda h, i, j: (0, 0)),
                  pl.BlockSpec((1, w), lambda h, i, j: (0, 0))],
        out_specs=pl.BlockSpec((t, w), lambda h, i, j: (i, h)),
        out_shape=jax.ShapeDtypeStruct((s, n_heads * w), jnp.bfloat16),
        scratch_shapes=[pltpu.VMEM((2, 1, t), jnp.float32),
                        pltpu.VMEM((2, 1, t), jnp.float32),
                        pltpu.VMEM((2, w, t), jnp.float32)],
        compiler_params=_cparams(("arbitrary", "arbitrary", "arbitrary")),
        name="diff_attention",
    )(qt, proj, vt, btiles, rel_bias, lam_vecs, sub_gain.reshape(1, w))


def _bf16_bits(x_bf16):
    return pltpu.bitcast(x_bf16.astype(jnp.float32), jnp.uint32)


def _route_kernel(x_ref, g_ref, rw_ref, h_ref, aff_ref):
    x = x_ref[...]
    y = x * lax.rsqrt(jnp.mean(x * x, axis=-1, keepdims=True) + NORM_EPS)
    hb = (y * g_ref[...]).astype(jnp.bfloat16)
    half = hb.shape[1] // 2
    h_ref[...] = (_bf16_bits(hb[:, :half]) >> 16) | _bf16_bits(hb[:, half:])
    logits = lax.dot_general(rw_ref[...], hb, (((1,), (1,)), ((), ())),
                             preferred_element_type=jnp.float32)
    e = jnp.exp(logits - jnp.max(logits, axis=0, keepdims=True))
    aff_ref[...] = e / jnp.sum(e, axis=0, keepdims=True)


def route(x, g, router_w_t):
    s, d = x.shape
    n_e = router_w_t.shape[0]
    ts = min(ROUTE_ROWS, s)
    return pl.pallas_call(
        _route_kernel,
        grid=(s // ts,),
        in_specs=[pl.BlockSpec((ts, d), lambda i: (i, 0)),
                  pl.BlockSpec((1, d), lambda i: (0, 0)),
                  pl.BlockSpec((n_e, d), lambda i: (0, 0))],
        out_specs=[pl.BlockSpec((ts, d // 2), lambda i: (i, 0)),
                   pl.BlockSpec((n_e, ts), lambda i: (0, i))],
        out_shape=[jax.ShapeDtypeStruct((s, d // 2), jnp.uint32),
                   jax.ShapeDtypeStruct((n_e, s), jnp.float32)],
        compiler_params=_cparams(("arbitrary",)),
        name="moe_route",
    )(x, g.reshape(1, d), router_w_t)


def _split3_bf16(x):
    hi = x.astype(jnp.bfloat16)
    r1 = x - hi.astype(jnp.float32)
    mid = r1.astype(jnp.bfloat16)
    lo = (r1 - mid.astype(jnp.float32)).astype(jnp.bfloat16)
    return hi, mid, lo


def _select_kernel(cap, aff_ref, idx_ref, gate_ref):
    a = aff_ref[0]
    nch = a.shape[0]
    bits = pltpu.bitcast(a, jnp.int32)

    def count(mask):
        c = jnp.sum(mask.astype(jnp.float32), axis=1, keepdims=True)
        return jnp.sum(c, axis=0, keepdims=True)

    thr = jnp.zeros((1, 1), jnp.int32)
    for b in range(30, -1, -1):
        cand = thr | (1 << b)
        thr = jnp.where(count(bits >= cand) >= cap, cand, thr)

    li = lax.broadcasted_iota(jnp.int32, (LANES, LANES), 0)
    lj = lax.broadcasted_iota(jnp.int32, (LANES, LANES), 1)
    tri_incl = (li <= lj).astype(jnp.bfloat16)
    ci = lax.broadcasted_iota(jnp.int32, (nch, nch), 0)
    cj = lax.broadcasted_iota(jnp.int32, (nch, nch), 1)
    before_rows = (cj < ci).astype(jnp.bfloat16)
    before_cols = (ci < cj).astype(jnp.bfloat16)
    ones8 = jnp.ones((8, LANES), jnp.bfloat16)

    def prefix(mask_f32):
        mb = mask_f32.astype(jnp.bfloat16)
        incl = jnp.dot(mb, tri_incl, preferred_element_type=jnp.float32)
        start_col = jnp.sum(jnp.dot(before_rows, mb, preferred_element_type=jnp.float32),
                            axis=1, keepdims=True)
        tot = lax.dot_general(ones8, mb, (((1,), (1,)), ((), ())), preferred_element_type=jnp.float32)
        start_lane = jnp.dot(tot.astype(jnp.bfloat16), before_cols,
                             preferred_element_type=jnp.float32)[0:1]
        return incl, start_col, start_lane

    gt = bits > thr
    eq = bits == thr
    need = cap - count(gt)
    eq_f = eq.astype(jnp.float32)
    incl_eq, start_eq, _ = prefix(eq_f)
    keep = jnp.logical_and(eq, (incl_eq - eq_f + start_eq) < need)
    sel_f = jnp.logical_or(gt, keep).astype(jnp.float32)
    incl, _, start_lane = prefix(sel_f)

    slot = lax.broadcasted_iota(jnp.int32, (cap, 1), 0).astype(jnp.float32)
    chunk = jnp.sum((start_lane <= slot).astype(jnp.float32), axis=1, keepdims=True) - 1.0
    chunk_iota = lax.broadcasted_iota(jnp.int32, (cap, nch), 1).astype(jnp.float32)
    onehot = chunk_iota == chunk
    chunk_start = jnp.sum(jnp.where(onehot, start_lane, 0.0), axis=1, keepdims=True)
    onehot_b = onehot.astype(jnp.bfloat16)
    rank = slot - chunk_start
    incl_rows = jnp.dot(onehot_b, incl.astype(jnp.bfloat16), preferred_element_type=jnp.float32)
    lane = jnp.sum((incl_rows <= rank).astype(jnp.float32), axis=1, keepdims=True)
    idx_ref[0] = (chunk * LANES + lane).astype(jnp.int32)

    hi, mid, lo = _split3_bf16(a)
    arow = (jnp.dot(onehot_b, hi, preferred_element_type=jnp.float32)
            + jnp.dot(onehot_b, mid, preferred_element_type=jnp.float32)
            + jnp.dot(onehot_b, lo, preferred_element_type=jnp.float32))
    lane_iota = lax.broadcasted_iota(jnp.int32, (cap, LANES), 1).astype(jnp.float32)
    gate_ref[0] = jnp.sum(jnp.where(lane_iota == lane, arow, 0.0), axis=1, keepdims=True)


def select(aff_t, cap):
    n_e, s = aff_t.shape
    nch = s // LANES
    return pl.pallas_call(
        functools.partial(_select_kernel, cap),
        grid=(n_e,),
        in_specs=[pl.BlockSpec((1, nch, LANES), lambda e: (e, 0, 0))],
        out_specs=[pl.BlockSpec((1, cap, 1), lambda e: (e, 0, 0)),
                   pl.BlockSpec((1, cap, 1), lambda e: (e, 0, 0))],
        out_shape=[jax.ShapeDtypeStruct((n_e, cap, 1), jnp.int32),
                   jax.ShapeDtypeStruct((n_e, cap, 1), jnp.float32)],
        compiler_params=_cparams(("arbitrary",)),
        name="moe_select",
    )(aff_t.reshape(n_e, nch, LANES))


def _row_copy_loop(hbm, vmem, sem, idx_ref, base, n_rows, to_vmem, wait):
    def body(r, c):
        tok = idx_ref[base + r]
        hbm_row = hbm.at[pl.ds(tok, 1)]
        vmem_row = vmem.at[pl.ds(r, 1)]
        copy = (pltpu.make_async_copy(hbm_row, vmem_row, sem) if to_vmem
                else pltpu.make_async_copy(vmem_row, hbm_row, sem))
        if wait:
            copy.wait()
        else:
            copy.start()
        return c

    lax.fori_loop(0, n_rows, body, 0)


def _row_copies(hbm, vmem, sem, idx_ref, base, n_rows, to_vmem):
    _row_copy_loop(hbm, vmem, sem, idx_ref, base, n_rows, to_vmem, wait=False)
    _row_copy_loop(hbm, vmem, sem, idx_ref, base, n_rows, to_vmem, wait=True)


def _expert_kernel(n_k, idx_ref, h_hbm, gate_ref, wg_ref, wu_ref, wd_ref, y_ref,
                   xs, g_acc, u_acc, act, sems):
    blk = pl.program_id(0)
    j = pl.program_id(1)
    n_blk = pl.num_programs(0)
    rows = xs.shape[1]
    kc = wg_ref.shape[1]
    slot = blk % 2

    def gather(b, s, wait):
        _row_copy_loop(h_hbm, xs.at[s], sems.at[s], idx_ref, b * rows, rows, True, wait)

    @pl.when(j == 0)
    def _():
        @pl.when(blk == 0)
        def _():
            gather(0, 0, wait=False)

        gather(blk, slot, wait=True)

        @pl.when(blk + 1 < n_blk)
        def _():
            gather(blk + 1, 1 - slot, wait=False)

        g_acc[...] = jnp.zeros(g_acc.shape, jnp.float32)
        u_acc[...] = jnp.zeros(u_acc.shape, jnp.float32)

    half = n_k // 2
    for jj in range(n_k):
        @pl.when(j == jj)
        def _():
            words = xs[slot, :, (jj % half) * kc:(jj % half + 1) * kc]
            bits = (words << 16) if jj < half else (words & jnp.uint32(0xFFFF0000))
            xk = pltpu.bitcast(bits, jnp.float32).astype(jnp.bfloat16)
            g_acc[...] += jnp.dot(xk, wg_ref[0].astype(jnp.bfloat16), preferred_element_type=jnp.float32)
            u_acc[...] += jnp.dot(xk, wu_ref[0].astype(jnp.bfloat16), preferred_element_type=jnp.float32)

    @pl.when(j == n_k - 1)
    def _():
        g = g_acc[...]
        act[...] = ((g * (1.0 / (1.0 + jnp.exp(-g)))) * u_acc[...]).astype(act.dtype)

    @pl.when(j >= n_k)
    def _():
        y = jnp.dot(act[...], wd_ref[0].astype(jnp.bfloat16), preferred_element_type=jnp.float32)
        y_ref[...] = y * gate_ref[...]


def expert_ffn(h_packed, idx_flat, gate_col, w_gate, w_up, w_down, layer):
    d = 2 * h_packed.shape[1]
    _, n_e, _, ff = w_gate.shape
    n_slots = idx_flat.shape[0]
    cap = n_slots // n_e
    rows = min(FFN_ROWS, cap)
    blocks_per_e = cap // rows
    kc = min(FFN_KCHUNK, d // 2)
    n_k = d // kc
    n_n = d // kc
    grid_spec = pltpu.PrefetchScalarGridSpec(
        num_scalar_prefetch=1,
        grid=(n_slots // rows, n_k + n_n),
        in_specs=[pl.BlockSpec(memory_space=pl.ANY),
                  pl.BlockSpec((rows, 1), lambda b, j, idx: (b, 0)),
                  pl.BlockSpec((None, 1, kc, ff),
                               lambda b, j, idx: (layer, b // blocks_per_e, jnp.minimum(j, n_k - 1), 0)),
                  pl.BlockSpec((None, 1, kc, ff),
                               lambda b, j, idx: (layer, b // blocks_per_e, jnp.minimum(j, n_k - 1), 0)),
                  pl.BlockSpec((None, 1, ff, kc),
                               lambda b, j, idx: (layer, b // blocks_per_e, 0, jnp.maximum(j - n_k, 0)))],
        out_specs=pl.BlockSpec((rows, kc), lambda b, j, idx: (b, jnp.maximum(j - n_k, 0))),
        scratch_shapes=[pltpu.VMEM((2, rows, d // 2), jnp.uint32),
                        pltpu.VMEM((rows, ff), jnp.float32),
                        pltpu.VMEM((rows, ff), jnp.float32),
                        pltpu.VMEM((rows, ff), jnp.bfloat16),
                        pltpu.SemaphoreType.DMA((2,))],
    )
    return pl.pallas_call(
        functools.partial(_expert_kernel, n_k),
        grid_spec=grid_spec,
        out_shape=jax.ShapeDtypeStruct((n_slots, d), jnp.float32),
        compiler_params=_cparams(("arbitrary", "arbitrary")),
        name="moe_expert_ffn",
    )(idx_flat, h_packed, gate_col, w_gate, w_up, w_down)


def _combine_kernel(idx_ref, x_in_hbm, y_ref, x_hbm, stage, sem_in, sem_out):
    del x_in_hbm
    blk = pl.program_id(0)
    rows = stage.shape[0]
    _row_copies(x_hbm, stage, sem_in, idx_ref, blk * rows, rows, True)
    stage[...] += y_ref[...]
    _row_copies(x_hbm, stage, sem_out, idx_ref, blk * rows, rows, False)


def combine(x, y_slots, idx_flat, n_e):
    s, d = x.shape
    n_slots = idx_flat.shape[0]
    cap = n_slots // n_e
    rows = min(COMBINE_ROWS, cap)
    grid_spec = pltpu.PrefetchScalarGridSpec(
        num_scalar_prefetch=1,
        grid=(n_slots // rows,),
        in_specs=[pl.BlockSpec(memory_space=pl.ANY),
                  pl.BlockSpec((rows, d), lambda b, idx: (b, 0))],
        out_specs=pl.BlockSpec(memory_space=pl.ANY),
        scratch_shapes=[pltpu.VMEM((rows, d), jnp.float32),
                        pltpu.SemaphoreType.DMA(()),
                        pltpu.SemaphoreType.DMA(())],
    )
    return pl.pallas_call(
        _combine_kernel,
        grid_spec=grid_spec,
        out_shape=jax.ShapeDtypeStruct((s, d), jnp.float32),
        input_output_aliases={1: 0},
        compiler_params=_cparams(("arbitrary",)),
        name="moe_combine",
    )(idx_flat, x, y_slots)


def expert_choice_ffn(x, norm_g, router_w, w_gate, w_up, w_down, layer):
    s, d = x.shape
    n_e = w_gate.shape[1]
    cap = EC_CAPACITY_FACTOR * s // n_e
    h_packed, aff_t = route(x, norm_g, router_w.T.astype(jnp.bfloat16))
    idx, gate = select(aff_t, cap)
    idx_flat = idx.reshape(n_e * cap)
    y = expert_ffn(h_packed, idx_flat, gate.reshape(n_e * cap, 1), w_gate, w_up, w_down, layer)
    return combine(x, y, idx_flat, n_e)


def kernel(x, attn_norm, ffn_norm, final_norm, gqa_w_in, gqa_q_norm, gqa_k_norm, gqa_w_out,
           diff_w_in, diff_lambda_q1, diff_lambda_k1, diff_lambda_q2, diff_lambda_k2,
           diff_sub_norm, diff_w_out, rel_bias, router_w, expert_w_gate, expert_w_up,
           expert_w_down):
    b, s, d = x.shape
    assert b == 1
    bf = jnp.bfloat16
    depth = attn_norm.shape[0]
    n_q = d // HEAD_DIM
    n_kv = n_q // GQA_GROUP
    n_diff = d // DIFF_V_DIM
    diff_t = min(DIFF_T, s)
    btiles = bias_tiles(rel_bias, diff_t) if depth > 1 else None

    xs = x.reshape(s, d)
    rope_tables = _rope_tables(s)
    for i in range(depth):
        j = i // N_MIXERS
        if i % N_MIXERS == 0:
            h = rmsnorm(xs, attn_norm[i], bf)
            proj = matmul(h, gqa_w_in, j, jnp.float32)
            qt = head_prep(proj, 0, n_q, rope_tables, gqa_q_norm[j], scale=HEAD_DIM ** -0.5 * LOG2E, transpose=True)
            k = head_prep(proj, n_q, n_kv, rope_tables, gqa_k_norm[j])
            vt = head_prep(proj, n_q + n_kv, n_kv, transpose=True)
            o = gqa_attention(qt, k, vt)
            xs = matmul(o, gqa_w_out, j, jnp.float32, residual=xs)
        else:
            lambda_init = 0.8 - 0.6 * math.exp(-0.3 * i)
            dqk = n_diff * 2 * DIFF_HEAD_DIM
            h = rmsnorm(xs, attn_norm[i], bf)
            proj = matmul(h, diff_w_in, j, bf, col_scale=(dqk, DIFF_HEAD_DIM ** -0.5 * LOG2E))
            qt = head_prep(proj, 0, 2 * n_diff, transpose=True).reshape(n_diff, DIFF_V_DIM, s)
            vt = head_prep(proj, 4 * n_diff, 2 * n_diff, transpose=True).reshape(n_diff, DIFF_V_DIM, s)
            lam_vecs = jnp.stack([diff_lambda_q1[j], diff_lambda_k1[j], diff_lambda_q2[j], diff_lambda_k2[j]])
            o = diff_attention(qt, proj, vt, btiles, rel_bias, lam_vecs, diff_sub_norm[j], lambda_init)
            xs = matmul(o, diff_w_out, j, jnp.float32, residual=xs)
        xs = expert_choice_ffn(xs, ffn_norm[i], router_w[i], expert_w_gate, expert_w_up, expert_w_down, i)
    return rmsnorm(xs, final_norm, jnp.float32).reshape(b, s, d)
```

```python
import functools
import math

import numpy as np
import jax
import jax.numpy as jnp
from jax import lax
from jax.experimental import pallas as pl
from jax.experimental.pallas import tpu as pltpu

HEAD_DIM = 128
GQA_GROUP = 4
DIFF_HEAD_DIM = 128
DIFF_V_DIM = 2 * DIFF_HEAD_DIM
EC_CAPACITY_FACTOR = 2
NUM_BUCKETS = 32
MAX_DISTANCE = 128
GRID_W = 64
ROPE_THETA = 10000.0
NORM_EPS = 1e-6
N_MIXERS = 2

LANES = 128
VMEM_LIMIT_BYTES = 56 * 1024 * 1024

NORM_ROWS = 256
MM_TM, MM_TN = 1024, 512
PREP_ROWS = 1024
GQA_TQ, GQA_TK = 512, 4096
DIFF_T = 1024
STAGE_KEYS = 1024
SCORE_LOOKAHEAD = 1
ROUTE_ROWS = 512
FFN_ROWS = 1024
FFN_KCHUNK = 512
WEIGHT_BUFFERS = 3
ROW_DMA_UNROLL = 8
COMBINE_ROWS = 512
NEG_BIG = -1e30
LOG2E = 1.4426950408889634


def _cparams(semantics):
    return pltpu.CompilerParams(dimension_semantics=semantics, vmem_limit_bytes=VMEM_LIMIT_BYTES)


def _rmsnorm_kernel(x_ref, g_ref, o_ref):
    x = x_ref[...]
    y = x * lax.rsqrt(jnp.mean(x * x, axis=-1, keepdims=True) + NORM_EPS)
    o_ref[...] = (y * g_ref[...]).astype(o_ref.dtype)


def rmsnorm(x, g, out_dtype):
    s, d = x.shape
    return pl.pallas_call(
        _rmsnorm_kernel,
        grid=(s // NORM_ROWS,),
        in_specs=[pl.BlockSpec((NORM_ROWS, d), lambda i: (i, 0)),
                  pl.BlockSpec((1, d), lambda i: (0, 0))],
        out_specs=pl.BlockSpec((NORM_ROWS, d), lambda i: (i, 0)),
        out_shape=jax.ShapeDtypeStruct((s, d), out_dtype),
        compiler_params=_cparams(("arbitrary",)),
        name="rmsnorm",
    )(x, g.reshape(1, d))


def _mm_kernel(col_scale, has_residual, a_ref, w_ref, *rest):
    o_ref = rest[-1]
    acc = jnp.dot(a_ref[...], w_ref[...].astype(jnp.bfloat16), preferred_element_type=jnp.float32)
    if col_scale is not None:
        n_tiles, factor = col_scale
        acc = acc * jnp.where(pl.program_id(1) < n_tiles, factor, 1.0).astype(jnp.float32)
    if has_residual:
        acc = rest[0][...] + acc
    o_ref[...] = acc.astype(o_ref.dtype)


def matmul(a, w_stack, layer, out_dtype, residual=None, col_scale=None):
    m, k = a.shape
    n = w_stack.shape[2]
    tm, tn = min(MM_TM, m), min(MM_TN, n)
    in_specs = [pl.BlockSpec((tm, k), lambda i, j: (i, 0)),
                pl.BlockSpec((None, k, tn), lambda i, j: (layer, 0, j))]
    args = [a, w_stack]
    if residual is not None:
        in_specs.append(pl.BlockSpec((tm, tn), lambda i, j: (i, j)))
        args.append(residual)
    if col_scale is not None:
        assert col_scale[0] % tn == 0
        col_scale = (col_scale[0] // tn, col_scale[1])
    return pl.pallas_call(
        functools.partial(_mm_kernel, col_scale, residual is not None),
        grid=(m // tm, n // tn),
        in_specs=in_specs,
        out_specs=pl.BlockSpec((tm, tn), lambda i, j: (i, j)),
        out_shape=jax.ShapeDtypeStruct((m, n), out_dtype),
        compiler_params=_cparams(("arbitrary", "arbitrary")),
        name="proj_matmul",
    )(*args)


def _rope_tables(s):
    half = HEAD_DIM // 2
    rows = s // GRID_W
    row = jnp.repeat(jnp.arange(rows, dtype=jnp.int32), GRID_W)
    col = jnp.tile(jnp.arange(GRID_W, dtype=jnp.int32), rows)
    freq = ROPE_THETA ** (-jnp.arange(0, half, 2, dtype=jnp.float32) / half)
    ang_r = row.astype(jnp.float32)[:, None] * freq[None, :]
    ang_c = col.astype(jnp.float32)[:, None] * freq[None, :]
    cos_t = jnp.concatenate([jnp.cos(ang_r), jnp.cos(ang_r), jnp.cos(ang_c), jnp.cos(ang_c)], axis=-1)
    sin_t = jnp.concatenate([-jnp.sin(ang_r), jnp.sin(ang_r), -jnp.sin(ang_c), jnp.sin(ang_c)], axis=-1)
    return cos_t, sin_t


def _head_prep_kernel(rope, transpose, scale, p_ref, *refs):
    o_ref = refs[-1]
    r = p_ref[...].astype(jnp.float32)
    if rope:
        cos_ref, sin_ref, g_ref = refs[:3]
        y = r * lax.rsqrt(jnp.mean(r * r, axis=-1, keepdims=True) + NORM_EPS)
        y = y * g_ref[...]
        quarter = HEAD_DIM // 4
        lane = lax.broadcasted_iota(jnp.int32, y.shape, 1)
        first = (lane & (2 * quarter - 1)) < quarter
        partner = jnp.where(first,
                            pltpu.roll(y, HEAD_DIM - quarter, axis=1),
                            pltpu.roll(y, quarter, axis=1))
        r = y * cos_ref[...] + partner * sin_ref[...]
    if scale is not None:
        r = r * scale
    if transpose:
        o_ref[0] = r.T.astype(o_ref.dtype)
    else:
        o_ref[...] = r.astype(o_ref.dtype)


def head_prep(proj, first_head, n_heads, tables=None, gain=None, scale=None, transpose=False):
    s = proj.shape[0]
    ts = min(PREP_ROWS, s)
    rope = tables is not None
    in_specs = [pl.BlockSpec((ts, HEAD_DIM), lambda i, h: (i, first_head + h))]
    args = [proj]
    if rope:
        in_specs += [pl.BlockSpec((ts, HEAD_DIM), lambda i, h: (i, 0)),
                     pl.BlockSpec((ts, HEAD_DIM), lambda i, h: (i, 0)),
                     pl.BlockSpec((1, HEAD_DIM), lambda i, h: (0, 0))]
        args += [tables[0], tables[1], gain.reshape(1, HEAD_DIM)]
    if transpose:
        out_spec = pl.BlockSpec((1, HEAD_DIM, ts), lambda i, h: (h, 0, i))
        out_shape = jax.ShapeDtypeStruct((n_heads, HEAD_DIM, s), jnp.bfloat16)
    else:
        out_spec = pl.BlockSpec((ts, HEAD_DIM), lambda i, h: (i, h))
        out_shape = jax.ShapeDtypeStruct((s, n_heads * HEAD_DIM), jnp.bfloat16)
    return pl.pallas_call(
        functools.partial(_head_prep_kernel, rope, transpose, scale),
        grid=(s // ts, n_heads),
        in_specs=in_specs,
        out_specs=out_spec,
        out_shape=out_shape,
        compiler_params=_cparams(("arbitrary", "arbitrary")),
        name="head_prep",
    )(*args)


def _gqa_attn_kernel(qt_ref, k_ref, vt_ref, o_ref, m_sc, l_sc, acc_sc):
    ki = pl.program_id(2)
    last = pl.num_programs(2) - 1

    @pl.when(ki == 0)
    def _():
        m_sc[...] = jnp.full(m_sc.shape, NEG_BIG, jnp.float32)
        l_sc[...] = jnp.zeros(l_sc.shape, jnp.float32)
        acc_sc[...] = jnp.zeros(acc_sc.shape, jnp.float32)

    tk = k_ref.shape[0]
    kc = min(STAGE_KEYS, tk)
    stages = [(g, c) for g in range(GQA_GROUP) for c in range(tk // kc)]

    def scores(stage):
        g, c = stage
        return jnp.dot(k_ref[c * kc:(c + 1) * kc, :], qt_ref[g], preferred_element_type=jnp.float32)

    pending = [scores(stage) for stage in stages[:SCORE_LOOKAHEAD]]
    for n, (g, c) in enumerate(stages):
        if n + SCORE_LOOKAHEAD < len(stages):
            pending.append(scores(stages[n + SCORE_LOOKAHEAD]))
        st = pending.pop(0)
        m_prev = m_sc[g]
        m_new = jnp.maximum(m_prev, jnp.max(st, axis=0, keepdims=True))
        alpha = jnp.exp2(m_prev - m_new)
        p = jnp.exp2(st - m_new)
        l_sc[g] = alpha * l_sc[g] + jnp.sum(p, axis=0, keepdims=True)
        vt = vt_ref[0, :, c * kc:(c + 1) * kc]
        acc_sc[g] = alpha * acc_sc[g] + jnp.dot(vt, p.astype(vt.dtype), preferred_element_type=jnp.float32)
        m_sc[g] = m_new

    @pl.when(ki == last)
    def _():
        for g in range(GQA_GROUP):
            o_ref[:, g * HEAD_DIM:(g + 1) * HEAD_DIM] = (acc_sc[g] / l_sc[g]).T.astype(o_ref.dtype)


def gqa_attention(qt, k, vt):
    n_q, _, s = qt.shape
    n_kv = vt.shape[0]
    tq, tk = min(GQA_TQ, s), min(GQA_TK, s)
    gw = GQA_GROUP * HEAD_DIM
    return pl.pallas_call(
        _gqa_attn_kernel,
        grid=(n_kv, s // tq, s // tk),
        in_specs=[pl.BlockSpec((GQA_GROUP, HEAD_DIM, tq), lambda h, i, j: (h, 0, i)),
                  pl.BlockSpec((tk, HEAD_DIM), lambda h, i, j: (j, h)),
                  pl.BlockSpec((1, HEAD_DIM, tk), lambda h, i, j: (h, 0, j))],
        out_specs=pl.BlockSpec((tq, gw), lambda h, i, j: (i, h)),
        out_shape=jax.ShapeDtypeStruct((s, n_q * HEAD_DIM), jnp.bfloat16),
        scratch_shapes=[pltpu.VMEM((GQA_GROUP, 1, tq), jnp.float32),
                        pltpu.VMEM((GQA_GROUP, 1, tq), jnp.float32),
                        pltpu.VMEM((GQA_GROUP, HEAD_DIM, tq), jnp.float32)],
        compiler_params=_cparams(("arbitrary", "arbitrary", "arbitrary")),
        name="gqa_attention",
    )(qt, k, vt)


def _t5_bucket(rel):
    nb = NUM_BUCKETS // 2
    max_exact = nb // 2
    n = jnp.abs(rel)
    large = max_exact + (jnp.log(jnp.maximum(n, 1).astype(jnp.float32) / max_exact)
                         / math.log(MAX_DISTANCE / max_exact) * (nb - max_exact)).astype(jnp.int32)
    large = jnp.minimum(large, nb - 1)
    return jnp.where(rel > 0, nb, 0) + jnp.where(n < max_exact, n, large)


def _bucket_saturation_distance():
    nb = NUM_BUCKETS // 2
    max_exact = nb // 2
    n = np.arange(1, 4 * MAX_DISTANCE, dtype=np.float64)
    large = max_exact + np.floor(np.log(n / max_exact) / math.log(MAX_DISTANCE / max_exact) * (nb - max_exact))
    unsat = np.nonzero(np.minimum(large, nb - 1) < nb - 1)[0]
    return int(n[unsat[-1]]) + 2


def _bias_tile_kernel(t, rows, rb_ref, o_ref):
    h = pl.program_id(0)
    d = pl.program_id(1)
    offset = (d - 1) * t

    nb = NUM_BUCKETS // 2
    win = min(rows + 2 * LANES, t)

    def body(c, carry):
        r0 = pl.multiple_of(c * rows, rows)
        w0 = pl.multiple_of(jnp.clip(r0 + offset - LANES, 0, t - win), LANES)
        if win < t:
            query_all = lax.broadcasted_iota(jnp.int32, (rows, t), 1)
            o_ref[0, 0, pl.ds(r0, rows), :] = jnp.where(query_all < w0, rb_ref[NUM_BUCKETS - 1, h] * LOG2E,
                                                        rb_ref[nb - 1, h] * LOG2E)
        key = lax.broadcasted_iota(jnp.int32, (rows, win), 0) + r0
        query = lax.broadcasted_iota(jnp.int32, (rows, win), 1) + w0
        bucket = _t5_bucket(key - query + offset)
        bias = jnp.zeros((rows, win), jnp.float32)
        for b in range(NUM_BUCKETS):
            bias = jnp.where(bucket == b, rb_ref[b, h] * LOG2E, bias)
        o_ref[0, 0, pl.ds(r0, rows), pl.ds(w0, win)] = bias
        return carry

    lax.fori_loop(0, t // rows, body, 0)


def bias_tiles(rel_bias, t):
    n_heads = rel_bias.shape[1]
    rows = min(LANES, t)
    assert _bucket_saturation_distance() <= LANES
    return pl.pallas_call(
        functools.partial(_bias_tile_kernel, t, rows),
        grid=(n_heads, 3),
        in_specs=[pl.BlockSpec(memory_space=pltpu.SMEM)],
        out_specs=pl.BlockSpec((1, 1, t, t), lambda h, d: (h, d, 0, 0)),
        out_shape=jax.ShapeDtypeStruct((n_heads, 3, t, t), jnp.float32),
        compiler_params=_cparams(("arbitrary", "arbitrary")),
        name="t5_bias_tiles",
    )(rel_bias)


def _diff_attn_kernel(lambda_init, qt_ref, k_ref, vt_ref, b_ref, rb_ref, lam_ref, sg_ref, o_ref,
                      m_sc, l_sc, acc_sc):
    h = pl.program_id(0)
    qi = pl.program_id(1)
    ki = pl.program_id(2)
    last = pl.num_programs(2) - 1
    nb = NUM_BUCKETS // 2

    @pl.when(ki == 0)
    def _():
        m_sc[...] = jnp.full(m_sc.shape, NEG_BIG, jnp.float32)
        l_sc[...] = jnp.zeros(l_sc.shape, jnp.float32)
        acc_sc[...] = jnp.zeros(acc_sc.shape, jnp.float32)

    def update(bias_tile, bias_const):
        tk = k_ref.shape[0]
        kc = min(STAGE_KEYS // 2, tk)
        stages = [(mp, c) for mp in range(2) for c in range(tk // kc)]

        def scores(stage):
            mp, c = stage
            rows = slice(mp * DIFF_HEAD_DIM, (mp + 1) * DIFF_HEAD_DIM)
            return jnp.dot(k_ref[c * kc:(c + 1) * kc, rows], qt_ref[0, rows, :],
                           preferred_element_type=jnp.float32)

        pending = [scores(stage) for stage in stages[:SCORE_LOOKAHEAD]]
        for n, (mp, c) in enumerate(stages):
            if n + SCORE_LOOKAHEAD < len(stages):
                pending.append(scores(stages[n + SCORE_LOOKAHEAD]))
            st = pending.pop(0)
            m_prev = m_sc[mp]
            if bias_tile:
                st = st + b_ref[0, 0, c * kc:(c + 1) * kc, :]
                m_new = jnp.maximum(m_prev, jnp.max(st, axis=0, keepdims=True))
                shift = m_new
            else:
                m_new = jnp.maximum(m_prev, jnp.max(st, axis=0, keepdims=True) + bias_const)
                shift = m_new - bias_const
            alpha = jnp.exp2(m_prev - m_new)
            p = jnp.exp2(st - shift)
            l_sc[mp] = alpha * l_sc[mp] + jnp.sum(p, axis=0, keepdims=True)
            vt = vt_ref[0, :, c * kc:(c + 1) * kc]
            acc_sc[mp] = alpha * acc_sc[mp] + jnp.dot(vt, p.astype(vt.dtype), preferred_element_type=jnp.float32)
            m_sc[mp] = m_new

    near = jnp.abs(ki - qi) <= 1

    @pl.when(near)
    def _():
        update(True, None)

    @pl.when(jnp.logical_not(near))
    def _():
        update(False, jnp.where(ki < qi, rb_ref[nb - 1, h], rb_ref[NUM_BUCKETS - 1, h]) * LOG2E)

    @pl.when(ki == last)
    def _():
        lv = lam_ref[...]
        lam = (jnp.exp(jnp.sum(lv[0:1] * lv[1:2], axis=1, keepdims=True))
               - jnp.exp(jnp.sum(lv[2:3] * lv[3:4], axis=1, keepdims=True)) + lambda_init)
        o = (acc_sc[0] / l_sc[0] - lam * (acc_sc[1] / l_sc[1])).T
        y = o * lax.rsqrt(jnp.mean(o * o, axis=-1, keepdims=True) + NORM_EPS)
        o_ref[...] = ((y * sg_ref[...]) * (1.0 - lambda_init)).astype(o_ref.dtype)


def diff_attention(qt, proj, vt, btiles, rel_bias, lam_vecs, sub_gain, lambda_init):
    n_heads, w, s = qt.shape
    t = btiles.shape[-1]
    assert t + 1 >= _bucket_saturation_distance()
    return pl.pallas_call(
        functools.partial(_diff_attn_kernel, lambda_init),
        grid=(n_heads, s // t, s // t),
        in_specs=[pl.BlockSpec((1, w, t), lambda h, i, j: (h, 0, i)),
                  pl.BlockSpec((t, w), lambda h, i, j: (j, n_heads + h)),
                  pl.BlockSpec((1, w, t), lambda h, i, j: (h, 0, j)),
                  pl.BlockSpec((1, 1, t, t), lambda h, i, j: (h, jnp.clip(j - i + 1, 0, 2), 0, 0)),
                  pl.BlockSpec(memory_space=pltpu.SMEM),
                  pl.BlockSpec((4, DIFF_HEAD_DIM), lambda h, i, j: (0, 0)),
                  pl.BlockSpec((1, w), lambda h, i, j: (0, 0))],
        out_specs=pl.BlockSpec((t, w), lambda h, i, j: (i, h)),
        out_shape=jax.ShapeDtypeStruct((s, n_heads * w), jnp.bfloat16),
        scratch_shapes=[pltpu.VMEM((2, 1, t), jnp.float32),
                        pltpu.VMEM((2, 1, t), jnp.float32),
                        pltpu.VMEM((2, w, t), jnp.float32)],
        compiler_params=_cparams(("arbitrary", "arbitrary", "arbitrary")),
        name="diff_attention",
    )(qt, proj, vt, btiles, rel_bias, lam_vecs, sub_gain.reshape(1, w))


def _bf16_bits(x_bf16):
    return pltpu.bitcast(x_bf16.astype(jnp.float32), jnp.uint32)


def _route_kernel(x_ref, g_ref, rw_ref, h_ref, aff_ref):
    x = x_ref[...]
    y = x * lax.rsqrt(jnp.mean(x * x, axis=-1, keepdims=True) + NORM_EPS)
    hb = (y * g_ref[...]).astype(jnp.bfloat16)
    half = hb.shape[1] // 2
    h_ref[...] = (_bf16_bits(hb[:, :half]) >> 16) | _bf16_bits(hb[:, half:])
    logits = lax.dot_general(rw_ref[...], hb, (((1,), (1,)), ((), ())),
                             preferred_element_type=jnp.float32)
    e = jnp.exp(logits - jnp.max(logits, axis=0, keepdims=True))
    aff_ref[...] = e / jnp.sum(e, axis=0, keepdims=True)


def route(x, g, router_w_t):
    s, d = x.shape
    n_e = router_w_t.shape[0]
    ts = min(ROUTE_ROWS, s)
    return pl.pallas_call(
        _route_kernel,
        grid=(s // ts,),
        in_specs=[pl.BlockSpec((ts, d), lambda i: (i, 0)),
                  pl.BlockSpec((1, d), lambda i: (0, 0)),
                  pl.BlockSpec((n_e, d), lambda i: (0, 0))],
        out_specs=[pl.BlockSpec((ts, d // 2), lambda i: (i, 0)),
                   pl.BlockSpec((n_e, ts), lambda i: (0, i))],
        out_shape=[jax.ShapeDtypeStruct((s, d // 2), jnp.uint32),
                   jax.ShapeDtypeStruct((n_e, s), jnp.float32)],
        compiler_params=_cparams(("arbitrary",)),
        name="moe_route",
    )(x, g.reshape(1, d), router_w_t)


def _split3_bf16(x):
    hi = x.astype(jnp.bfloat16)
    r1 = x - hi.astype(jnp.float32)
    mid = r1.astype(jnp.bfloat16)
    lo = (r1 - mid.astype(jnp.float32)).astype(jnp.bfloat16)
    return hi, mid, lo


def _select_kernel(cap, aff_ref, idx_ref, gate_ref):
    a = aff_ref[0]
    nch = a.shape[0]
    bits = pltpu.bitcast(a, jnp.int32)

    def count(mask):
        c = jnp.sum(mask.astype(jnp.float32), axis=1, keepdims=True)
        return jnp.sum(c, axis=0, keepdims=True)

    thr = jnp.zeros((1, 1), jnp.int32)
    for b in range(30, -1, -1):
        cand = thr | (1 << b)
        thr = jnp.where(count(bits >= cand) >= cap, cand, thr)

    li = lax.broadcasted_iota(jnp.int32, (LANES, LANES), 0)
    lj = lax.broadcasted_iota(jnp.int32, (LANES, LANES), 1)
    tri_incl = (li <= lj).astype(jnp.bfloat16)
    ci = lax.broadcasted_iota(jnp.int32, (nch, nch), 0)
    cj = lax.broadcasted_iota(jnp.int32, (nch, nch), 1)
    before_rows = (cj < ci).astype(jnp.bfloat16)
    before_cols = (ci < cj).astype(jnp.bfloat16)
    ones8 = jnp.ones((8, LANES), jnp.bfloat16)

    def prefix(mask_f32):
        mb = mask_f32.astype(jnp.bfloat16)
        incl = jnp.dot(mb, tri_incl, preferred_element_type=jnp.float32)
        start_col = jnp.sum(jnp.dot(before_rows, mb, preferred_element_type=jnp.float32),
                            axis=1, keepdims=True)
        tot = lax.dot_general(ones8, mb, (((1,), (1,)), ((), ())), preferred_element_type=jnp.float32)
        start_lane = jnp.dot(tot.astype(jnp.bfloat16), before_cols,
                             preferred_element_type=jnp.float32)[0:1]
        return incl, start_col, start_lane

    gt = bits > thr
    eq = bits == thr
    need = cap - count(gt)
    eq_f = eq.astype(jnp.float32)
    incl_eq, start_eq, _ = prefix(eq_f)
    keep = jnp.logical_and(eq, (incl_eq - eq_f + start_eq) < need)
    sel_f = jnp.logical_or(gt, keep).astype(jnp.float32)
    incl, _, start_lane = prefix(sel_f)

    slot = lax.broadcasted_iota(jnp.int32, (cap, 1), 0).astype(jnp.float32)
    chunk = jnp.sum((start_lane <= slot).astype(jnp.float32), axis=1, keepdims=True) - 1.0
    chunk_iota = lax.broadcasted_iota(jnp.int32, (cap, nch), 1).astype(jnp.float32)
    onehot = chunk_iota == chunk
    chunk_start = jnp.sum(jnp.where(onehot, start_lane, 0.0), axis=1, keepdims=True)
    onehot_b = onehot.astype(jnp.bfloat16)
    rank = slot - chunk_start
    incl_rows = jnp.dot(onehot_b, incl.astype(jnp.bfloat16), preferred_element_type=jnp.float32)
    lane = jnp.sum((incl_rows <= rank).astype(jnp.float32), axis=1, keepdims=True)
    idx_ref[0] = (chunk * LANES + lane).astype(jnp.int32)

    hi, mid, lo = _split3_bf16(a)
    arow = (jnp.dot(onehot_b, hi, preferred_element_type=jnp.float32)
            + jnp.dot(onehot_b, mid, preferred_element_type=jnp.float32)
            + jnp.dot(onehot_b, lo, preferred_element_type=jnp.float32))
    lane_iota = lax.broadcasted_iota(jnp.int32, (cap, LANES), 1).astype(jnp.float32)
    gate_ref[0] = jnp.sum(jnp.where(lane_iota == lane, arow, 0.0), axis=1, keepdims=True)


def select(aff_t, cap):
    n_e, s = aff_t.shape
    nch = s // LANES
    return pl.pallas_call(
        functools.partial(_select_kernel, cap),
        grid=(n_e,),
        in_specs=[pl.BlockSpec((1, nch, LANES), lambda e: (e, 0, 0))],
        out_specs=[pl.BlockSpec((1, cap, 1), lambda e: (e, 0, 0)),
                   pl.BlockSpec((1, cap, 1), lambda e: (e, 0, 0))],
        out_shape=[jax.ShapeDtypeStruct((n_e, cap, 1), jnp.int32),
                   jax.ShapeDtypeStruct((n_e, cap, 1), jnp.float32)],
        compiler_params=_cparams(("arbitrary",)),
        name="moe_select",
    )(aff_t.reshape(n_e, nch, LANES))


def _row_copy_loop(hbm, vmem, sem, idx_ref, base, n_rows, to_vmem, wait):
    def body(r, c):
        tok = idx_ref[base + r]
        hbm_row = hbm.at[pl.ds(tok, 1)]
        vmem_row = vmem.at[pl.ds(r, 1)]
        copy = (pltpu.make_async_copy(hbm_row, vmem_row, sem) if to_vmem
                else pltpu.make_async_copy(vmem_row, hbm_row, sem))
        if wait:
            copy.wait()
        else:
            copy.start()
        return c

    lax.fori_loop(0, n_rows, body, 0, unroll=ROW_DMA_UNROLL)


def _row_copies(hbm, vmem, sem, idx_ref, base, n_rows, to_vmem):
    _row_copy_loop(hbm, vmem, sem, idx_ref, base, n_rows, to_vmem, wait=False)
    _row_copy_loop(hbm, vmem, sem, idx_ref, base, n_rows, to_vmem, wait=True)


def _expert_kernel(n_k, blocks_per_e, layer, idx_ref, h_hbm, gate_ref, wg_hbm, wu_hbm, wd_hbm, y_ref,
                   xs, kbuf, nbuf, g_acc, u_acc, act, sems, ksems, nsems):
    blk = pl.program_id(0)
    j = pl.program_id(1)
    n_blk = pl.num_programs(0)
    rows = xs.shape[1]
    kc = kbuf.shape[2]
    slot = blk % 2
    n_units = n_blk * n_k

    def gather(b, s, wait):
        _row_copy_loop(h_hbm, xs.at[s], sems.at[s], idx_ref, b * rows, rows, True, wait)

    def gate_up_copies(u):
        e = (u // n_k) // blocks_per_e
        r0 = pl.multiple_of((u % n_k) * kc, kc)
        s = u % WEIGHT_BUFFERS
        return (pltpu.make_async_copy(wg_hbm.at[layer, e, pl.ds(r0, kc), :], kbuf.at[s, 0], ksems.at[s]),
                pltpu.make_async_copy(wu_hbm.at[layer, e, pl.ds(r0, kc), :], kbuf.at[s, 1], ksems.at[s]))

    def down_copy(u):
        e = (u // n_k) // blocks_per_e
        c0 = pl.multiple_of((u % n_k) * kc, kc)
        s = u % WEIGHT_BUFFERS
        return pltpu.make_async_copy(wd_hbm.at[layer, e, :, pl.ds(c0, kc)], nbuf.at[s], nsems.at[s])

    def start_gate_up(u):
        @pl.when(u < n_units)
        def _():
            for copy in gate_up_copies(u):
                copy.start()

    def start_down(u):
        @pl.when(u < n_units)
        def _():
            down_copy(u).start()

    @pl.when(jnp.logical_and(blk == 0, j == 0))
    def _():
        for u in range(WEIGHT_BUFFERS - 1):
            start_gate_up(jnp.int32(u))
            start_down(jnp.int32(u))

    @pl.when(j == 0)
    def _():
        @pl.when(blk == 0)
        def _():
            gather(0, 0, wait=False)

        gather(blk, slot, wait=True)

        @pl.when(blk + 1 < n_blk)
        def _():
            gather(blk + 1, 1 - slot, wait=False)

        g_acc[...] = jnp.zeros(g_acc.shape, jnp.float32)
        u_acc[...] = jnp.zeros(u_acc.shape, jnp.float32)

    half = n_k // 2
    for jj in range(n_k):
        @pl.when(j == jj)
        def _():
            unit = blk * n_k + jj
            start_gate_up(unit + (WEIGHT_BUFFERS - 1))
            for copy in gate_up_copies(unit):
                copy.wait()
            ws = unit % WEIGHT_BUFFERS
            words = xs[slot, :, (jj % half) * kc:(jj % half + 1) * kc]
            bits = (words << 16) if jj < half else (words & jnp.uint32(0xFFFF0000))
            xk = pltpu.bitcast(bits, jnp.float32).astype(jnp.bfloat16)
            g_acc[...] += jnp.dot(xk, kbuf[ws, 0].astype(jnp.bfloat16), preferred_element_type=jnp.float32)
            u_acc[...] += jnp.dot(xk, kbuf[ws, 1].astype(jnp.bfloat16), preferred_element_type=jnp.float32)

    @pl.when(j == n_k - 1)
    def _():
        g = g_acc[...]
        act[...] = ((g * (1.0 / (1.0 + jnp.exp(-g)))) * u_acc[...]).astype(act.dtype)

    @pl.when(j >= n_k)
    def _():
        unit = blk * n_k + (j - n_k)
        start_down(unit + (WEIGHT_BUFFERS - 1))
        down_copy(unit).wait()
        wd = nbuf[unit % WEIGHT_BUFFERS].astype(jnp.bfloat16)
        y = jnp.dot(act[...], wd, preferred_element_type=jnp.float32)
        y_ref[...] = y * gate_ref[...]


def expert_ffn(h_packed, idx_flat, gate_col, w_gate, w_up, w_down, layer):
    d = 2 * h_packed.shape[1]
    _, n_e, _, ff = w_gate.shape
    n_slots = idx_flat.shape[0]
    cap = n_slots // n_e
    rows = min(FFN_ROWS, cap)
    blocks_per_e = cap // rows
    kc = min(FFN_KCHUNK, d // 2)
    n_k = d // kc
    n_n = d // kc
    grid_spec = pltpu.PrefetchScalarGridSpec(
        num_scalar_prefetch=1,
        grid=(n_slots // rows, n_k + n_n),
        in_specs=[pl.BlockSpec(memory_space=pl.ANY),
                  pl.BlockSpec((rows, 1), lambda b, j, idx: (b, 0)),
                  pl.BlockSpec(memory_space=pl.ANY),
                  pl.BlockSpec(memory_space=pl.ANY),
                  pl.BlockSpec(memory_space=pl.ANY)],
        out_specs=pl.BlockSpec((rows, kc), lambda b, j, idx: (b, jnp.maximum(j - n_k, 0))),
        scratch_shapes=[pltpu.VMEM((2, rows, d // 2), jnp.uint32),
                        pltpu.VMEM((WEIGHT_BUFFERS, 2, kc, ff), jnp.float32),
                        pltpu.VMEM((WEIGHT_BUFFERS, ff, kc), jnp.float32),
                        pltpu.VMEM((rows, ff), jnp.float32),
                        pltpu.VMEM((rows, ff), jnp.float32),
                        pltpu.VMEM((rows, ff), jnp.bfloat16),
                        pltpu.SemaphoreType.DMA((2,)),
                        pltpu.SemaphoreType.DMA((WEIGHT_BUFFERS,)),
                        pltpu.SemaphoreType.DMA((WEIGHT_BUFFERS,))],
    )
    return pl.pallas_call(
        functools.partial(_expert_kernel, n_k, blocks_per_e, layer),
        grid_spec=grid_spec,
        out_shape=jax.ShapeDtypeStruct((n_slots, d), jnp.float32),
        compiler_params=_cparams(("arbitrary", "arbitrary")),
        name="moe_expert_ffn",
    )(idx_flat, h_packed, gate_col, w_gate, w_up, w_down)


def _combine_kernel(idx_ref, x_in_hbm, y_ref, x_hbm, stage, sem_in, sem_out):
    del x_in_hbm
    blk = pl.program_id(0)
    rows = stage.shape[0]
    _row_copies(x_hbm, stage, sem_in, idx_ref, blk * rows, rows, True)
    stage[...] += y_ref[...]
    _row_copies(x_hbm, stage, sem_out, idx_ref, blk * rows, rows, False)


def combine(x, y_slots, idx_flat, n_e):
    s, d = x.shape
    n_slots = idx_flat.shape[0]
    cap = n_slots // n_e
    rows = min(COMBINE_ROWS, cap)
    grid_spec = pltpu.PrefetchScalarGridSpec(
        num_scalar_prefetch=1,
        grid=(n_slots // rows,),
        in_specs=[pl.BlockSpec(memory_space=pl.ANY),
                  pl.BlockSpec((rows, d), lambda b, idx: (b, 0))],
        out_specs=pl.BlockSpec(memory_space=pl.ANY),
        scratch_shapes=[pltpu.VMEM((rows, d), jnp.float32),
                        pltpu.SemaphoreType.DMA(()),
                        pltpu.SemaphoreType.DMA(())],
    )
    return pl.pallas_call(
        _combine_kernel,
        grid_spec=grid_spec,
        out_shape=jax.ShapeDtypeStruct((s, d), jnp.float32),
        input_output_aliases={1: 0},
        compiler_params=_cparams(("arbitrary",)),
        name="moe_combine",
    )(idx_flat, x, y_slots)


def expert_choice_ffn(x, norm_g, router_w, w_gate, w_up, w_down, layer):
    s, d = x.shape
    n_e = w_gate.shape[1]
    cap = EC_CAPACITY_FACTOR * s // n_e
    h_packed, aff_t = route(x, norm_g, router_w.T.astype(jnp.bfloat16))
    idx, gate = select(aff_t, cap)
    idx_flat = idx.reshape(n_e * cap)
    y = expert_ffn(h_packed, idx_flat, gate.reshape(n_e * cap, 1), w_gate, w_up, w_down, layer)
    return combine(x, y, idx_flat, n_e)


def kernel(x, attn_norm, ffn_norm, final_norm, gqa_w_in, gqa_q_norm, gqa_k_norm, gqa_w_out,
           diff_w_in, diff_lambda_q1, diff_lambda_k1, diff_lambda_q2, diff_lambda_k2,
           diff_sub_norm, diff_w_out, rel_bias, router_w, expert_w_gate, expert_w_up,
           expert_w_down):
    b, s, d = x.shape
    assert b == 1
    bf = jnp.bfloat16
    depth = attn_norm.shape[0]
    n_q = d // HEAD_DIM
    n_kv = n_q // GQA_GROUP
    n_diff = d // DIFF_V_DIM
    diff_t = min(DIFF_T, s)
    btiles = bias_tiles(rel_bias, diff_t) if depth > 1 else None

    xs = x.reshape(s, d)
    rope_tables = _rope_tables(s)
    for i in range(depth):
        j = i // N_MIXERS
        if i % N_MIXERS == 0:
            h = rmsnorm(xs, attn_norm[i], bf)
            proj = matmul(h, gqa_w_in, j, jnp.float32)
            qt = head_prep(proj, 0, n_q, rope_tables, gqa_q_norm[j], scale=HEAD_DIM ** -0.5 * LOG2E, transpose=True)
            k = head_prep(proj, n_q, n_kv, rope_tables, gqa_k_norm[j])
            vt = head_prep(proj, n_q + n_kv, n_kv, transpose=True)
            o = gqa_attention(qt, k, vt)
            xs = matmul(o, gqa_w_out, j, jnp.float32, residual=xs)
        else:
            lambda_init = 0.8 - 0.6 * math.exp(-0.3 * i)
            dqk = n_diff * 2 * DIFF_HEAD_DIM
            h = rmsnorm(xs, attn_norm[i], bf)
            proj = matmul(h, diff_w_in, j, bf, col_scale=(dqk, DIFF_HEAD_DIM ** -0.5 * LOG2E))
            qt = head_prep(proj, 0, 2 * n_diff, transpose=True).reshape(n_diff, DIFF_V_DIM, s)
            vt = head_prep(proj, 4 * n_diff, 2 * n_diff, transpose=True).reshape(n_diff, DIFF_V_DIM, s)
            lam_vecs = jnp.stack([diff_lambda_q1[j], diff_lambda_k1[j], diff_lambda_q2[j], diff_lambda_k2[j]])
            o = diff_attention(qt, proj, vt, btiles, rel_bias, lam_vecs, diff_sub_norm[j], lambda_init)
            xs = matmul(o, diff_w_out, j, jnp.float32, residual=xs)
        xs = expert_choice_ffn(xs, ffn_norm[i], router_w[i], expert_w_gate, expert_w_up, expert_w_down, i)
    return rmsnorm(xs, final_norm, jnp.float32).reshape(b, s, d)
```

```python
import functools
import math

import numpy as np
import jax
import jax.numpy as jnp
from jax import lax
from jax.experimental import pallas as pl
from jax.experimental.pallas import tpu as pltpu

HEAD_DIM = 128
GQA_GROUP = 4
DIFF_HEAD_DIM = 128
DIFF_V_DIM = 2 * DIFF_HEAD_DIM
EC_CAPACITY_FACTOR = 2
NUM_BUCKETS = 32
MAX_DISTANCE = 128
GRID_W = 64
ROPE_THETA = 10000.0
NORM_EPS = 1e-6
N_MIXERS = 2

LANES = 128
VMEM_LIMIT_BYTES = 56 * 1024 * 1024

NORM_ROWS = 512
MM_TM, MM_TN = 1024, 512
PREP_ROWS = 2048
GQA_TQ, GQA_TK = 512, 4096
DIFF_T = 1024
DIFF_HEADS_PER_STEP = 2
STAGE_KEYS = 1024
SCORE_LOOKAHEAD = 1
ROUTE_ROWS = 512
FFN_ROWS = 1024
FFN_KCHUNK = 512
WEIGHT_BUFFERS = 3
ROW_DMA_UNROLL = 8
COMBINE_ROWS = 512
NEG_BIG = -1e30
LOG2E = 1.4426950408889634


def _cparams(semantics):
    return pltpu.CompilerParams(dimension_semantics=semantics, vmem_limit_bytes=VMEM_LIMIT_BYTES)


def _rmsnorm_kernel(x_ref, g_ref, o_ref):
    x = x_ref[...]
    y = x * lax.rsqrt(jnp.mean(x * x, axis=-1, keepdims=True) + NORM_EPS)
    o_ref[...] = (y * g_ref[...]).astype(o_ref.dtype)


def rmsnorm(x, g, out_dtype):
    s, d = x.shape
    return pl.pallas_call(
        _rmsnorm_kernel,
        grid=(s // NORM_ROWS,),
        in_specs=[pl.BlockSpec((NORM_ROWS, d), lambda i: (i, 0)),
                  pl.BlockSpec((1, d), lambda i: (0, 0))],
        out_specs=pl.BlockSpec((NORM_ROWS, d), lambda i: (i, 0)),
        out_shape=jax.ShapeDtypeStruct((s, d), out_dtype),
        compiler_params=_cparams(("arbitrary",)),
        name="rmsnorm",
    )(x, g.reshape(1, d))


def _mm_kernel(col_scale, has_residual, a_ref, w_ref, *rest):
    o_ref = rest[-1]
    acc = jnp.dot(a_ref[...], w_ref[...].astype(jnp.bfloat16), preferred_element_type=jnp.float32)
    if col_scale is not None:
        n_tiles, factor = col_scale
        acc = acc * jnp.where(pl.program_id(1) < n_tiles, factor, 1.0).astype(jnp.float32)
    if has_residual:
        acc = rest[0][...] + acc
    o_ref[...] = acc.astype(o_ref.dtype)


def matmul(a, w_stack, layer, out_dtype, residual=None, col_scale=None):
    m, k = a.shape
    n = w_stack.shape[2]
    tm, tn = min(MM_TM, m), min(MM_TN, n)
    in_specs = [pl.BlockSpec((tm, k), lambda i, j: (i, 0)),
                pl.BlockSpec((None, k, tn), lambda i, j: (layer, 0, j))]
    args = [a, w_stack]
    if residual is not None:
        in_specs.append(pl.BlockSpec((tm, tn), lambda i, j: (i, j)))
        args.append(residual)
    if col_scale is not None:
        assert col_scale[0] % tn == 0
        col_scale = (col_scale[0] // tn, col_scale[1])
    return pl.pallas_call(
        functools.partial(_mm_kernel, col_scale, residual is not None),
        grid=(m // tm, n // tn),
        in_specs=in_specs,
        out_specs=pl.BlockSpec((tm, tn), lambda i, j: (i, j)),
        out_shape=jax.ShapeDtypeStruct((m, n), out_dtype),
        compiler_params=_cparams(("arbitrary", "arbitrary")),
        name="proj_matmul",
    )(*args)


def _rope_tables(s):
    half = HEAD_DIM // 2
    rows = s // GRID_W
    row = jnp.repeat(jnp.arange(rows, dtype=jnp.int32), GRID_W)
    col = jnp.tile(jnp.arange(GRID_W, dtype=jnp.int32), rows)
    freq = ROPE_THETA ** (-jnp.arange(0, half, 2, dtype=jnp.float32) / half)
    ang_r = row.astype(jnp.float32)[:, None] * freq[None, :]
    ang_c = col.astype(jnp.float32)[:, None] * freq[None, :]
    cos_t = jnp.concatenate([jnp.cos(ang_r), jnp.cos(ang_r), jnp.cos(ang_c), jnp.cos(ang_c)], axis=-1)
    sin_t = jnp.concatenate([-jnp.sin(ang_r), jnp.sin(ang_r), -jnp.sin(ang_c), jnp.sin(ang_c)], axis=-1)
    return cos_t, sin_t


def _head_prep_kernel(rope, transpose, scale, p_ref, *refs):
    o_ref = refs[-1]
    r = p_ref[...].astype(jnp.float32)
    if rope:
        cos_ref, sin_ref, g_ref = refs[:3]
        y = r * lax.rsqrt(jnp.mean(r * r, axis=-1, keepdims=True) + NORM_EPS)
        y = y * g_ref[...]
        quarter = HEAD_DIM // 4
        lane = lax.broadcasted_iota(jnp.int32, y.shape, 1)
        first = (lane & (2 * quarter - 1)) < quarter
        partner = jnp.where(first,
                            pltpu.roll(y, HEAD_DIM - quarter, axis=1),
                            pltpu.roll(y, quarter, axis=1))
        r = y * cos_ref[...] + partner * sin_ref[...]
    if scale is not None:
        r = r * scale
    if transpose:
        o_ref[0] = r.T.astype(o_ref.dtype)
    else:
        o_ref[...] = r.astype(o_ref.dtype)


def head_prep(proj, first_head, n_heads, tables=None, gain=None, scale=None, transpose=False):
    s = proj.shape[0]
    ts = min(PREP_ROWS, s)
    rope = tables is not None
    in_specs = [pl.BlockSpec((ts, HEAD_DIM), lambda i, h: (i, first_head + h))]
    args = [proj]
    if rope:
        in_specs += [pl.BlockSpec((ts, HEAD_DIM), lambda i, h: (i, 0)),
                     pl.BlockSpec((ts, HEAD_DIM), lambda i, h: (i, 0)),
                     pl.BlockSpec((1, HEAD_DIM), lambda i, h: (0, 0))]
        args += [tables[0], tables[1], gain.reshape(1, HEAD_DIM)]
    if transpose:
        out_spec = pl.BlockSpec((1, HEAD_DIM, ts), lambda i, h: (h, 0, i))
        out_shape = jax.ShapeDtypeStruct((n_heads, HEAD_DIM, s), jnp.bfloat16)
    else:
        out_spec = pl.BlockSpec((ts, HEAD_DIM), lambda i, h: (i, h))
        out_shape = jax.ShapeDtypeStruct((s, n_heads * HEAD_DIM), jnp.bfloat16)
    return pl.pallas_call(
        functools.partial(_head_prep_kernel, rope, transpose, scale),
        grid=(s // ts, n_heads),
        in_specs=in_specs,
        out_specs=out_spec,
        out_shape=out_shape,
        compiler_params=_cparams(("arbitrary", "arbitrary")),
        name="head_prep",
    )(*args)


def _gqa_attn_kernel(qt_ref, k_ref, vt_ref, o_ref, m_sc, l_sc, acc_sc):
    ki = pl.program_id(2)
    last = pl.num_programs(2) - 1

    @pl.when(ki == 0)
    def _():
        m_sc[...] = jnp.full(m_sc.shape, NEG_BIG, jnp.float32)
        l_sc[...] = jnp.zeros(l_sc.shape, jnp.float32)
        acc_sc[...] = jnp.zeros(acc_sc.shape, jnp.float32)

    tk = k_ref.shape[0]
    kc = min(STAGE_KEYS, tk)
    stages = [(g, c) for g in range(GQA_GROUP) for c in range(tk // kc)]

    def scores(stage):
        g, c = stage
        return jnp.dot(k_ref[c * kc:(c + 1) * kc, :], qt_ref[g], preferred_element_type=jnp.float32)

    pending = [scores(stage) for stage in stages[:SCORE_LOOKAHEAD]]
    for n, (g, c) in enumerate(stages):
        if n + SCORE_LOOKAHEAD < len(stages):
            pending.append(scores(stages[n + SCORE_LOOKAHEAD]))
        st = pending.pop(0)
        m_prev = m_sc[g]
        m_new = jnp.maximum(m_prev, jnp.max(st, axis=0, keepdims=True))
        alpha = jnp.exp2(m_prev - m_new)
        p = jnp.exp2(st - m_new)
        l_sc[g] = alpha * l_sc[g] + jnp.sum(p, axis=0, keepdims=True)
        vt = vt_ref[0, :, c * kc:(c + 1) * kc]
        acc_sc[g] = alpha * acc_sc[g] + jnp.dot(vt, p.astype(vt.dtype), preferred_element_type=jnp.float32)
        m_sc[g] = m_new

    @pl.when(ki == last)
    def _():
        for g in range(GQA_GROUP):
            o_ref[:, g * HEAD_DIM:(g + 1) * HEAD_DIM] = (acc_sc[g] / l_sc[g]).T.astype(o_ref.dtype)


def gqa_attention(qt, k, vt):
    n_q, _, s = qt.shape
    n_kv = vt.shape[0]
    tq, tk = min(GQA_TQ, s), min(GQA_TK, s)
    gw = GQA_GROUP * HEAD_DIM
    return pl.pallas_call(
        _gqa_attn_kernel,
        grid=(n_kv, s // tq, s // tk),
        in_specs=[pl.BlockSpec((GQA_GROUP, HEAD_DIM, tq), lambda h, i, j: (h, 0, i)),
                  pl.BlockSpec((tk, HEAD_DIM), lambda h, i, j: (j, h)),
                  pl.BlockSpec((1, HEAD_DIM, tk), lambda h, i, j: (h, 0, j))],
        out_specs=pl.BlockSpec((tq, gw), lambda h, i, j: (i, h)),
        out_shape=jax.ShapeDtypeStruct((s, n_q * HEAD_DIM), jnp.bfloat16),
        scratch_shapes=[pltpu.VMEM((GQA_GROUP, 1, tq), jnp.float32),
                        pltpu.VMEM((GQA_GROUP, 1, tq), jnp.float32),
                        pltpu.VMEM((GQA_GROUP, HEAD_DIM, tq), jnp.float32)],
        compiler_params=_cparams(("arbitrary", "arbitrary", "arbitrary")),
        name="gqa_attention",
    )(qt, k, vt)


def _t5_bucket(rel):
    nb = NUM_BUCKETS // 2
    max_exact = nb // 2
    n = jnp.abs(rel)
    large = max_exact + (jnp.log(jnp.maximum(n, 1).astype(jnp.float32) / max_exact)
                         / math.log(MAX_DISTANCE / max_exact) * (nb - max_exact)).astype(jnp.int32)
    large = jnp.minimum(large, nb - 1)
    return jnp.where(rel > 0, nb, 0) + jnp.where(n < max_exact, n, large)


def _bucket_saturation_distance():
    nb = NUM_BUCKETS // 2
    max_exact = nb // 2
    n = np.arange(1, 4 * MAX_DISTANCE, dtype=np.float64)
    large = max_exact + np.floor(np.log(n / max_exact) / math.log(MAX_DISTANCE / max_exact) * (nb - max_exact))
    unsat = np.nonzero(np.minimum(large, nb - 1) < nb - 1)[0]
    return int(n[unsat[-1]]) + 2


def _bias_tile_kernel(t, rows, rb_ref, o_ref):
    h = pl.program_id(0)
    d = pl.program_id(1)
    offset = (d - 1) * t

    nb = NUM_BUCKETS // 2
    win = min(rows + 2 * LANES, t)

    def body(c, carry):
        r0 = pl.multiple_of(c * rows, rows)
        w0 = pl.multiple_of(jnp.clip(r0 + offset - LANES, 0, t - win), LANES)
        if win < t:
            query_all = lax.broadcasted_iota(jnp.int32, (rows, t), 1)
            o_ref[0, 0, pl.ds(r0, rows), :] = jnp.where(query_all < w0, rb_ref[NUM_BUCKETS - 1, h] * LOG2E,
                                                        rb_ref[nb - 1, h] * LOG2E)
        key = lax.broadcasted_iota(jnp.int32, (rows, win), 0) + r0
        query = lax.broadcasted_iota(jnp.int32, (rows, win), 1) + w0
        bucket = _t5_bucket(key - query + offset)
        bias = jnp.zeros((rows, win), jnp.float32)
        for b in range(NUM_BUCKETS):
            bias = jnp.where(bucket == b, rb_ref[b, h] * LOG2E, bias)
        o_ref[0, 0, pl.ds(r0, rows), pl.ds(w0, win)] = bias
        return carry

    lax.fori_loop(0, t // rows, body, 0)


def bias_tiles(rel_bias, t):
    n_heads = rel_bias.shape[1]
    rows = min(LANES, t)
    assert _bucket_saturation_distance() <= LANES
    return pl.pallas_call(
        functools.partial(_bias_tile_kernel, t, rows),
        grid=(n_heads, 3),
        in_specs=[pl.BlockSpec(memory_space=pltpu.SMEM)],
        out_specs=pl.BlockSpec((1, 1, t, t), lambda h, d: (h, d, 0, 0)),
        out_shape=jax.ShapeDtypeStruct((n_heads, 3, t, t), jnp.float32),
        compiler_params=_cparams(("arbitrary", "arbitrary")),
        name="t5_bias_tiles",
    )(rel_bias)


def _diff_attn_kernel(lambda_init, qt_ref, k_ref, vt_ref, b_ref, rb_ref, lam_ref, sg_ref, o_ref,
                      m_sc, l_sc, acc_sc):
    hg = pl.program_id(0)
    qi = pl.program_id(1)
    ki = pl.program_id(2)
    last = pl.num_programs(2) - 1
    nb = NUM_BUCKETS // 2

    @pl.when(ki == 0)
    def _():
        m_sc[...] = jnp.full(m_sc.shape, NEG_BIG, jnp.float32)
        l_sc[...] = jnp.zeros(l_sc.shape, jnp.float32)
        acc_sc[...] = jnp.zeros(acc_sc.shape, jnp.float32)

    n_local = qt_ref.shape[0]
    w = DIFF_V_DIM

    def update(bias_tile):
        tk = k_ref.shape[0]
        kc = min(STAGE_KEYS // 2, tk)
        stages = [(hh, mp, c) for hh in range(n_local) for mp in range(2) for c in range(tk // kc)]

        def scores(stage):
            hh, mp, c = stage
            rows = slice(mp * DIFF_HEAD_DIM, (mp + 1) * DIFF_HEAD_DIM)
            cols = slice(hh * w + mp * DIFF_HEAD_DIM, hh * w + (mp + 1) * DIFF_HEAD_DIM)
            return jnp.dot(k_ref[c * kc:(c + 1) * kc, cols], qt_ref[hh, rows, :],
                           preferred_element_type=jnp.float32)

        pending = [scores(stage) for stage in stages[:SCORE_LOOKAHEAD]]
        for n, (hh, mp, c) in enumerate(stages):
            if n + SCORE_LOOKAHEAD < len(stages):
                pending.append(scores(stages[n + SCORE_LOOKAHEAD]))
            st = pending.pop(0)
            a = hh * 2 + mp
            m_prev = m_sc[a]
            if bias_tile:
                st = st + b_ref[hh, 0, c * kc:(c + 1) * kc, :]
                m_new = jnp.maximum(m_prev, jnp.max(st, axis=0, keepdims=True))
                shift = m_new
            else:
                head = hg * n_local + hh
                bias_const = jnp.where(ki < qi, rb_ref[nb - 1, head], rb_ref[NUM_BUCKETS - 1, head]) * LOG2E
                m_new = jnp.maximum(m_prev, jnp.max(st, axis=0, keepdims=True) + bias_const)
                shift = m_new - bias_const
            alpha = jnp.exp2(m_prev - m_new)
            p = jnp.exp2(st - shift)
            l_sc[a] = alpha * l_sc[a] + jnp.sum(p, axis=0, keepdims=True)
            vt = vt_ref[hh, :, c * kc:(c + 1) * kc]
            acc_sc[a] = alpha * acc_sc[a] + jnp.dot(vt, p.astype(vt.dtype), preferred_element_type=jnp.float32)
            m_sc[a] = m_new

    near = jnp.abs(ki - qi) <= 1

    @pl.when(near)
    def _():
        update(True)

    @pl.when(jnp.logical_not(near))
    def _():
        update(False)

    @pl.when(ki == last)
    def _():
        lv = lam_ref[...]
        lam = (jnp.exp(jnp.sum(lv[0:1] * lv[1:2], axis=1, keepdims=True))
               - jnp.exp(jnp.sum(lv[2:3] * lv[3:4], axis=1, keepdims=True)) + lambda_init)
        for hh in range(n_local):
            a = 2 * hh
            o = (acc_sc[a] / l_sc[a] - lam * (acc_sc[a + 1] / l_sc[a + 1])).T
            y = o * lax.rsqrt(jnp.mean(o * o, axis=-1, keepdims=True) + NORM_EPS)
            o_ref[:, hh * w:(hh + 1) * w] = ((y * sg_ref[...]) * (1.0 - lambda_init)).astype(o_ref.dtype)


def diff_attention(qt, proj, vt, btiles, rel_bias, lam_vecs, sub_gain, lambda_init):
    n_heads, w, s = qt.shape
    t = btiles.shape[-1]
    assert t + 1 >= _bucket_saturation_distance()
    hp = min(DIFF_HEADS_PER_STEP, n_heads)
    n_groups = n_heads // hp
    return pl.pallas_call(
        functools.partial(_diff_attn_kernel, lambda_init),
        grid=(n_groups, s // t, s // t),
        in_specs=[pl.BlockSpec((hp, w, t), lambda h, i, j: (h, 0, i)),
                  pl.BlockSpec((t, hp * w), lambda h, i, j: (j, n_groups + h)),
                  pl.BlockSpec((hp, w, t), lambda h, i, j: (h, 0, j)),
                  pl.BlockSpec((hp, 1, t, t), lambda h, i, j: (h, jnp.clip(j - i + 1, 0, 2), 0, 0)),
                  pl.BlockSpec(memory_space=pltpu.SMEM),
                  pl.BlockSpec((4, DIFF_HEAD_DIM), lambda h, i, j: (0, 0)),
                  pl.BlockSpec((1, w), lambda h, i, j: (0, 0))],
        out_specs=pl.BlockSpec((t, hp * w), lambda h, i, j: (i, h)),
        out_shape=jax.ShapeDtypeStruct((s, n_heads * w), jnp.bfloat16),
        scratch_shapes=[pltpu.VMEM((2 * hp, 1, t), jnp.float32),
                        pltpu.VMEM((2 * hp, 1, t), jnp.float32),
                        pltpu.VMEM((2 * hp, w, t), jnp.float32)],
        compiler_params=_cparams(("arbitrary", "arbitrary", "arbitrary")),
        name="diff_attention",
    )(qt, proj, vt, btiles, rel_bias, lam_vecs, sub_gain.reshape(1, w))


def _bf16_bits(x_bf16):
    return pltpu.bitcast(x_bf16.astype(jnp.float32), jnp.uint32)


def _route_kernel(x_ref, g_ref, rw_ref, h_ref, aff_ref):
    x = x_ref[...]
    y = x * lax.rsqrt(jnp.mean(x * x, axis=-1, keepdims=True) + NORM_EPS)
    hb = (y * g_ref[...]).astype(jnp.bfloat16)
    half = hb.shape[1] // 2
    h_ref[...] = (_bf16_bits(hb[:, :half]) >> 16) | _bf16_bits(hb[:, half:])
    logits = lax.dot_general(rw_ref[...], hb, (((1,), (1,)), ((), ())),
                             preferred_element_type=jnp.float32)
    e = jnp.exp(logits - jnp.max(logits, axis=0, keepdims=True))
    aff_ref[...] = e / jnp.sum(e, axis=0, keepdims=True)


def route(x, g, router_w_t):
    s, d = x.shape
    n_e = router_w_t.shape[0]
    ts = min(ROUTE_ROWS, s)
    return pl.pallas_call(
        _route_kernel,
        grid=(s // ts,),
        in_specs=[pl.BlockSpec((ts, d), lambda i: (i, 0)),
                  pl.BlockSpec((1, d), lambda i: (0, 0)),
                  pl.BlockSpec((n_e, d), lambda i: (0, 0))],
        out_specs=[pl.BlockSpec((ts, d // 2), lambda i: (i, 0)),
                   pl.BlockSpec((n_e, ts), lambda i: (0, i))],
        out_shape=[jax.ShapeDtypeStruct((s, d // 2), jnp.uint32),
                   jax.ShapeDtypeStruct((n_e, s), jnp.float32)],
        compiler_params=_cparams(("arbitrary",)),
        name="moe_route",
    )(x, g.reshape(1, d), router_w_t)


def _split3_bf16(x):
    hi = x.astype(jnp.bfloat16)
    r1 = x - hi.astype(jnp.float32)
    mid = r1.astype(jnp.bfloat16)
    lo = (r1 - mid.astype(jnp.float32)).astype(jnp.bfloat16)
    return hi, mid, lo


def _select_kernel(cap, aff_ref, idx_ref, gate_ref):
    a = aff_ref[0]
    nch = a.shape[0]
    bits = pltpu.bitcast(a, jnp.int32)

    def count(mask):
        c = jnp.sum(mask.astype(jnp.float32), axis=1, keepdims=True)
        return jnp.sum(c, axis=0, keepdims=True)

    thr = jnp.zeros((1, 1), jnp.int32)
    for b in range(30, -1, -1):
        cand = thr | (1 << b)
        thr = jnp.where(count(bits >= cand) >= cap, cand, thr)

    li = lax.broadcasted_iota(jnp.int32, (LANES, LANES), 0)
    lj = lax.broadcasted_iota(jnp.int32, (LANES, LANES), 1)
    tri_incl = (li <= lj).astype(jnp.bfloat16)
    ci = lax.broadcasted_iota(jnp.int32, (nch, nch), 0)
    cj = lax.broadcasted_iota(jnp.int32, (nch, nch), 1)
    before_rows = (cj < ci).astype(jnp.bfloat16)
    before_cols = (ci < cj).astype(jnp.bfloat16)
    ones8 = jnp.ones((8, LANES), jnp.bfloat16)

    def prefix(mask_f32):
        mb = mask_f32.astype(jnp.bfloat16)
        incl = jnp.dot(mb, tri_incl, preferred_element_type=jnp.float32)
        start_col = jnp.sum(jnp.dot(before_rows, mb, preferred_element_type=jnp.float32),
                            axis=1, keepdims=True)
        tot = lax.dot_general(ones8, mb, (((1,), (1,)), ((), ())), preferred_element_type=jnp.float32)
        start_lane = jnp.dot(tot.astype(jnp.bfloat16), before_cols,
                             preferred_element_type=jnp.float32)[0:1]
        return incl, start_col, start_lane

    gt = bits > thr
    eq = bits == thr
    need = cap - count(gt)
    eq_f = eq.astype(jnp.float32)
    incl_eq, start_eq, _ = prefix(eq_f)
    keep = jnp.logical_and(eq, (incl_eq - eq_f + start_eq) < need)
    sel_f = jnp.logical_or(gt, keep).astype(jnp.float32)
    incl, _, start_lane = prefix(sel_f)

    slot = lax.broadcasted_iota(jnp.int32, (cap, 1), 0).astype(jnp.float32)
    chunk = jnp.sum((start_lane <= slot).astype(jnp.float32), axis=1, keepdims=True) - 1.0
    chunk_iota = lax.broadcasted_iota(jnp.int32, (cap, nch), 1).astype(jnp.float32)
    onehot = chunk_iota == chunk
    chunk_start = jnp.sum(jnp.where(onehot, start_lane, 0.0), axis=1, keepdims=True)
    onehot_b = onehot.astype(jnp.bfloat16)
    rank = slot - chunk_start
    incl_rows = jnp.dot(onehot_b, incl.astype(jnp.bfloat16), preferred_element_type=jnp.float32)
    lane = jnp.sum((incl_rows <= rank).astype(jnp.float32), axis=1, keepdims=True)
    idx_ref[0] = (chunk * LANES + lane).astype(jnp.int32)

    hi, mid, lo = _split3_bf16(a)
    arow = (jnp.dot(onehot_b, hi, preferred_element_type=jnp.float32)
            + jnp.dot(onehot_b, mid, preferred_element_type=jnp.float32)
            + jnp.dot(onehot_b, lo, preferred_element_type=jnp.float32))
    lane_iota = lax.broadcasted_iota(jnp.int32, (cap, LANES), 1).astype(jnp.float32)
    gate_ref[0] = jnp.sum(jnp.where(lane_iota == lane, arow, 0.0), axis=1, keepdims=True)


def select(aff_t, cap):
    n_e, s = aff_t.shape
    nch = s // LANES
    return pl.pallas_call(
        functools.partial(_select_kernel, cap),
        grid=(n_e,),
        in_specs=[pl.BlockSpec((1, nch, LANES), lambda e: (e, 0, 0))],
        out_specs=[pl.BlockSpec((1, cap, 1), lambda e: (e, 0, 0)),
                   pl.BlockSpec((1, cap, 1), lambda e: (e, 0, 0))],
        out_shape=[jax.ShapeDtypeStruct((n_e, cap, 1), jnp.int32),
                   jax.ShapeDtypeStruct((n_e, cap, 1), jnp.float32)],
        compiler_params=_cparams(("arbitrary",)),
        name="moe_select",
    )(aff_t.reshape(n_e, nch, LANES))


def _row_copy_loop(hbm, vmem, sem, idx_ref, base, n_rows, to_vmem, wait):
    def body(r, c):
        tok = idx_ref[base + r]
        hbm_row = hbm.at[pl.ds(tok, 1)]
        vmem_row = vmem.at[pl.ds(r, 1)]
        copy = (pltpu.make_async_copy(hbm_row, vmem_row, sem) if to_vmem
                else pltpu.make_async_copy(vmem_row, hbm_row, sem))
        if wait:
            copy.wait()
        else:
            copy.start()
        return c

    lax.fori_loop(0, n_rows, body, 0, unroll=ROW_DMA_UNROLL)


def _row_copies(hbm, vmem, sem, idx_ref, base, n_rows, to_vmem):
    _row_copy_loop(hbm, vmem, sem, idx_ref, base, n_rows, to_vmem, wait=False)
    _row_copy_loop(hbm, vmem, sem, idx_ref, base, n_rows, to_vmem, wait=True)


def _expert_kernel(n_k, blocks_per_e, layer, idx_ref, h_hbm, gate_ref, wg_hbm, wu_hbm, wd_hbm, y_ref,
                   xs, kbuf, nbuf, g_acc, u_acc, act, sems, ksems, nsems):
    blk = pl.program_id(0)
    j = pl.program_id(1)
    n_blk = pl.num_programs(0)
    rows = xs.shape[1]
    kc = kbuf.shape[2]
    slot = blk % 2
    n_units = n_blk * n_k

    def gather(b, s, wait):
        _row_copy_loop(h_hbm, xs.at[s], sems.at[s], idx_ref, b * rows, rows, True, wait)

    def gate_up_copies(u):
        e = (u // n_k) // blocks_per_e
        r0 = pl.multiple_of((u % n_k) * kc, kc)
        s = u % WEIGHT_BUFFERS
        return (pltpu.make_async_copy(wg_hbm.at[layer, e, pl.ds(r0, kc), :], kbuf.at[s, 0], ksems.at[s]),
                pltpu.make_async_copy(wu_hbm.at[layer, e, pl.ds(r0, kc), :], kbuf.at[s, 1], ksems.at[s]))

    def down_copy(u):
        e = (u // n_k) // blocks_per_e
        c0 = pl.multiple_of((u % n_k) * kc, kc)
        s = u % WEIGHT_BUFFERS
        return pltpu.make_async_copy(wd_hbm.at[layer, e, :, pl.ds(c0, kc)], nbuf.at[s], nsems.at[s])

    def start_gate_up(u):
        @pl.when(u < n_units)
        def _():
            for copy in gate_up_copies(u):
                copy.start()

    def start_down(u):
        @pl.when(u < n_units)
        def _():
            down_copy(u).start()

    @pl.when(jnp.logical_and(blk == 0, j == 0))
    def _():
        for u in range(WEIGHT_BUFFERS - 1):
            start_gate_up(jnp.int32(u))
            start_down(jnp.int32(u))

    @pl.when(j == 0)
    def _():
        @pl.when(blk == 0)
        def _():
            gather(0, 0, wait=False)

        gather(blk, slot, wait=True)

        @pl.when(blk + 1 < n_blk)
        def _():
            gather(blk + 1, 1 - slot, wait=False)

        g_acc[...] = jnp.zeros(g_acc.shape, jnp.float32)
        u_acc[...] = jnp.zeros(u_acc.shape, jnp.float32)

    half = n_k // 2
    for jj in range(n_k):
        @pl.when(j == jj)
        def _():
            unit = blk * n_k + jj
            start_gate_up(unit + (WEIGHT_BUFFERS - 1))
            for copy in gate_up_copies(unit):
                copy.wait()
            ws = unit % WEIGHT_BUFFERS
            words = xs[slot, :, (jj % half) * kc:(jj % half + 1) * kc]
            bits = (words << 16) if jj < half else (words & jnp.uint32(0xFFFF0000))
            xk = pltpu.bitcast(bits, jnp.float32).astype(jnp.bfloat16)
            g_acc[...] += jnp.dot(xk, kbuf[ws, 0].astype(jnp.bfloat16), preferred_element_type=jnp.float32)
            u_acc[...] += jnp.dot(xk, kbuf[ws, 1].astype(jnp.bfloat16), preferred_element_type=jnp.float32)

    @pl.when(j == n_k - 1)
    def _():
        g = g_acc[...]
        act[...] = ((g * (1.0 / (1.0 + jnp.exp(-g)))) * u_acc[...]).astype(act.dtype)

    @pl.when(j >= n_k)
    def _():
        unit = blk * n_k + (j - n_k)
        start_down(unit + (WEIGHT_BUFFERS - 1))
        down_copy(unit).wait()
        wd = nbuf[unit % WEIGHT_BUFFERS].astype(jnp.bfloat16)
        y = jnp.dot(act[...], wd, preferred_element_type=jnp.float32)
        y_ref[...] = y * gate_ref[...]


def expert_ffn(h_packed, idx_flat, gate_col, w_gate, w_up, w_down, layer):
    d = 2 * h_packed.shape[1]
    _, n_e, _, ff = w_gate.shape
    n_slots = idx_flat.shape[0]
    cap = n_slots // n_e
    rows = min(FFN_ROWS, cap)
    blocks_per_e = cap // rows
    kc = min(FFN_KCHUNK, d // 2)
    n_k = d // kc
    n_n = d // kc
    grid_spec = pltpu.PrefetchScalarGridSpec(
        num_scalar_prefetch=1,
        grid=(n_slots // rows, n_k + n_n),
        in_specs=[pl.BlockSpec(memory_space=pl.ANY),
                  pl.BlockSpec((rows, 1), lambda b, j, idx: (b, 0)),
                  pl.BlockSpec(memory_space=pl.ANY),
                  pl.BlockSpec(memory_space=pl.ANY),
                  pl.BlockSpec(memory_space=pl.ANY)],
        out_specs=pl.BlockSpec((rows, kc), lambda b, j, idx: (b, jnp.maximum(j - n_k, 0))),
        scratch_shapes=[pltpu.VMEM((2, rows, d // 2), jnp.uint32),
                        pltpu.VMEM((WEIGHT_BUFFERS, 2, kc, ff), jnp.float32),
                        pltpu.VMEM((WEIGHT_BUFFERS, ff, kc), jnp.float32),
                        pltpu.VMEM((rows, ff), jnp.float32),
                        pltpu.VMEM((rows, ff), jnp.float32),
                        pltpu.VMEM((rows, ff), jnp.bfloat16),
                        pltpu.SemaphoreType.DMA((2,)),
                        pltpu.SemaphoreType.DMA((WEIGHT_BUFFERS,)),
                        pltpu.SemaphoreType.DMA((WEIGHT_BUFFERS,))],
    )
    return pl.pallas_call(
        functools.partial(_expert_kernel, n_k, blocks_per_e, layer),
        grid_spec=grid_spec,
        out_shape=jax.ShapeDtypeStruct((n_slots, d), jnp.float32),
        compiler_params=_cparams(("arbitrary", "arbitrary")),
        name="moe_expert_ffn",
    )(idx_flat, h_packed, gate_col, w_gate, w_up, w_down)


def _combine_kernel(idx_ref, x_in_hbm, y_ref, x_hbm, stage, sem_in, sem_out):
    del x_in_hbm
    blk = pl.program_id(0)
    rows = stage.shape[0]
    _row_copies(x_hbm, stage, sem_in, idx_ref, blk * rows, rows, True)
    stage[...] += y_ref[...]
    _row_copies(x_hbm, stage, sem_out, idx_ref, blk * rows, rows, False)


def combine(x, y_slots, idx_flat, n_e):
    s, d = x.shape
    n_slots = idx_flat.shape[0]
    cap = n_slots // n_e
    rows = min(COMBINE_ROWS, cap)
    grid_spec = pltpu.PrefetchScalarGridSpec(
        num_scalar_prefetch=1,
        grid=(n_slots // rows,),
        in_specs=[pl.BlockSpec(memory_space=pl.ANY),
                  pl.BlockSpec((rows, d), lambda b, idx: (b, 0))],
        out_specs=pl.BlockSpec(memory_space=pl.ANY),
        scratch_shapes=[pltpu.VMEM((rows, d), jnp.float32),
                        pltpu.SemaphoreType.DMA(()),
                        pltpu.SemaphoreType.DMA(())],
    )
    return pl.pallas_call(
        _combine_kernel,
        grid_spec=grid_spec,
        out_shape=jax.ShapeDtypeStruct((s, d), jnp.float32),
        input_output_aliases={1: 0},
        compiler_params=_cparams(("arbitrary",)),
        name="moe_combine",
    )(idx_flat, x, y_slots)


def expert_choice_ffn(x, norm_g, router_w, w_gate, w_up, w_down, layer):
    s, d = x.shape
    n_e = w_gate.shape[1]
    cap = EC_CAPACITY_FACTOR * s // n_e
    h_packed, aff_t = route(x, norm_g, router_w.T.astype(jnp.bfloat16))
    idx, gate = select(aff_t, cap)
    idx_flat = idx.reshape(n_e * cap)
    y = expert_ffn(h_packed, idx_flat, gate.reshape(n_e * cap, 1), w_gate, w_up, w_down, layer)
    return combine(x, y, idx_flat, n_e)


def kernel(x, attn_norm, ffn_norm, final_norm, gqa_w_in, gqa_q_norm, gqa_k_norm, gqa_w_out,
           diff_w_in, diff_lambda_q1, diff_lambda_k1, diff_lambda_q2, diff_lambda_k2,
           diff_sub_norm, diff_w_out, rel_bias, router_w, expert_w_gate, expert_w_up,
           expert_w_down):
    b, s, d = x.shape
    assert b == 1
    bf = jnp.bfloat16
    depth = attn_norm.shape[0]
    n_q = d // HEAD_DIM
    n_kv = n_q // GQA_GROUP
    n_diff = d // DIFF_V_DIM
    diff_t = min(DIFF_T, s)
    btiles = bias_tiles(rel_bias, diff_t) if depth > 1 else None

    xs = x.reshape(s, d)
    rope_tables = _rope_tables(s)
    for i in range(depth):
        j = i // N_MIXERS
        if i % N_MIXERS == 0:
            h = rmsnorm(xs, attn_norm[i], bf)
            proj = matmul(h, gqa_w_in, j, jnp.float32)
            qt = head_prep(proj, 0, n_q, rope_tables, gqa_q_norm[j], scale=HEAD_DIM ** -0.5 * LOG2E, transpose=True)
            k = head_prep(proj, n_q, n_kv, rope_tables, gqa_k_norm[j])
            vt = head_prep(proj, n_q + n_kv, n_kv, transpose=True)
            o = gqa_attention(qt, k, vt)
            xs = matmul(o, gqa_w_out, j, jnp.float32, residual=xs)
        else:
            lambda_init = 0.8 - 0.6 * math.exp(-0.3 * i)
            dqk = n_diff * 2 * DIFF_HEAD_DIM
            h = rmsnorm(xs, attn_norm[i], bf)
            proj = matmul(h, diff_w_in, j, bf, col_scale=(dqk, DIFF_HEAD_DIM ** -0.5 * LOG2E))
            qt = head_prep(proj, 0, 2 * n_diff, transpose=True).reshape(n_diff, DIFF_V_DIM, s)
            vt = head_prep(proj, 4 * n_diff, 2 * n_diff, transpose=True).reshape(n_diff, DIFF_V_DIM, s)
            lam_vecs = jnp.stack([diff_lambda_q1[j], diff_lambda_k1[j], diff_lambda_q2[j], diff_lambda_k2[j]])
            o = diff_attention(qt, proj, vt, btiles, rel_bias, lam_vecs, diff_sub_norm[j], lambda_init)
            xs = matmul(o, diff_w_out, j, jnp.float32, residual=xs)
        xs = expert_choice_ffn(xs, ffn_norm[i], router_w[i], expert_w_gate, expert_w_up, expert_w_down, i)
    return rmsnorm(xs, final_norm, jnp.float32).reshape(b, s, d)
```

```python
import functools
import math

import numpy as np
import jax
import jax.numpy as jnp
from jax import lax
from jax.experimental import pallas as pl
from jax.experimental.pallas import tpu as pltpu

HEAD_DIM = 128
GQA_GROUP = 4
DIFF_HEAD_DIM = 128
DIFF_V_DIM = 2 * DIFF_HEAD_DIM
EC_CAPACITY_FACTOR = 2
NUM_BUCKETS = 32
MAX_DISTANCE = 128
GRID_W = 64
ROPE_THETA = 10000.0
NORM_EPS = 1e-6
N_MIXERS = 2

LANES = 128
VMEM_LIMIT_BYTES = 56 * 1024 * 1024

NORM_ROWS = 512
MM_TM, MM_TN = 1024, 512
PREP_ROWS = 2048
GQA_TQ, GQA_TK = 512, 4096
DIFF_T = 1024
DIFF_HEADS_PER_STEP = 2
STAGE_KEYS = 1024
SCORE_LOOKAHEAD = 1
ROUTE_ROWS = 512
FFN_ROWS = 1024
FFN_KCHUNK = 512
WEIGHT_BUFFERS = 3
ROW_DMA_UNROLL = 8
COMBINE_ROWS = 512
NEG_BIG = -1e30
LOG2E = 1.4426950408889634


def _cparams(semantics):
    return pltpu.CompilerParams(dimension_semantics=semantics, vmem_limit_bytes=VMEM_LIMIT_BYTES)


def _rmsnorm_kernel(x_ref, g_ref, o_ref):
    x = x_ref[...]
    y = x * lax.rsqrt(jnp.mean(x * x, axis=-1, keepdims=True) + NORM_EPS)
    o_ref[...] = (y * g_ref[...]).astype(o_ref.dtype)


def rmsnorm(x, g, out_dtype):
    s, d = x.shape
    return pl.pallas_call(
        _rmsnorm_kernel,
        grid=(s // NORM_ROWS,),
        in_specs=[pl.BlockSpec((NORM_ROWS, d), lambda i: (i, 0)),
                  pl.BlockSpec((1, d), lambda i: (0, 0))],
        out_specs=pl.BlockSpec((NORM_ROWS, d), lambda i: (i, 0)),
        out_shape=jax.ShapeDtypeStruct((s, d), out_dtype),
        compiler_params=_cparams(("arbitrary",)),
        name="rmsnorm",
    )(x, g.reshape(1, d))


def _mm_kernel(col_scale, has_residual, a_ref, w_ref, *rest):
    o_ref = rest[-1]
    acc = jnp.dot(a_ref[...], w_ref[...].astype(jnp.bfloat16), preferred_element_type=jnp.float32)
    if col_scale is not None:
        n_tiles, factor = col_scale
        acc = acc * jnp.where(pl.program_id(1) < n_tiles, factor, 1.0).astype(jnp.float32)
    if has_residual:
        acc = rest[0][...] + acc
    o_ref[...] = acc.astype(o_ref.dtype)


def matmul(a, w_stack, layer, out_dtype, residual=None, col_scale=None):
    m, k = a.shape
    n = w_stack.shape[2]
    tm, tn = min(MM_TM, m), min(MM_TN, n)
    in_specs = [pl.BlockSpec((tm, k), lambda i, j: (i, 0)),
                pl.BlockSpec((None, k, tn), lambda i, j: (layer, 0, j))]
    args = [a, w_stack]
    if residual is not None:
        in_specs.append(pl.BlockSpec((tm, tn), lambda i, j: (i, j)))
        args.append(residual)
    if col_scale is not None:
        assert col_scale[0] % tn == 0
        col_scale = (col_scale[0] // tn, col_scale[1])
    return pl.pallas_call(
        functools.partial(_mm_kernel, col_scale, residual is not None),
        grid=(m // tm, n // tn),
        in_specs=in_specs,
        out_specs=pl.BlockSpec((tm, tn), lambda i, j: (i, j)),
        out_shape=jax.ShapeDtypeStruct((m, n), out_dtype),
        compiler_params=_cparams(("arbitrary", "arbitrary")),
        name="proj_matmul",
    )(*args)


def _rope_tables(s):
    half = HEAD_DIM // 2
    rows = s // GRID_W
    row = jnp.repeat(jnp.arange(rows, dtype=jnp.int32), GRID_W)
    col = jnp.tile(jnp.arange(GRID_W, dtype=jnp.int32), rows)
    freq = ROPE_THETA ** (-jnp.arange(0, half, 2, dtype=jnp.float32) / half)
    ang_r = row.astype(jnp.float32)[:, None] * freq[None, :]
    ang_c = col.astype(jnp.float32)[:, None] * freq[None, :]
    cos_t = jnp.concatenate([jnp.cos(ang_r), jnp.cos(ang_r), jnp.cos(ang_c), jnp.cos(ang_c)], axis=-1)
    sin_t = jnp.concatenate([-jnp.sin(ang_r), jnp.sin(ang_r), -jnp.sin(ang_c), jnp.sin(ang_c)], axis=-1)
    return cos_t, sin_t


def _head_prep_kernel(rope, transpose, scale, p_ref, *refs):
    o_ref = refs[-1]
    r = p_ref[...].astype(jnp.float32)
    if rope:
        cos_ref, sin_ref, g_ref = refs[:3]
        y = r * lax.rsqrt(jnp.mean(r * r, axis=-1, keepdims=True) + NORM_EPS)
        y = y * g_ref[...]
        quarter = HEAD_DIM // 4
        lane = lax.broadcasted_iota(jnp.int32, y.shape, 1)
        first = (lane & (2 * quarter - 1)) < quarter
        partner = jnp.where(first,
                            pltpu.roll(y, HEAD_DIM - quarter, axis=1),
                            pltpu.roll(y, quarter, axis=1))
        r = y * cos_ref[...] + partner * sin_ref[...]
    if scale is not None:
        r = r * scale
    if transpose:
        o_ref[0] = r.T.astype(o_ref.dtype)
    else:
        o_ref[...] = r.astype(o_ref.dtype)


def head_prep(proj, first_head, n_heads, tables=None, gain=None, scale=None, transpose=False):
    s = proj.shape[0]
    ts = min(PREP_ROWS, s)
    rope = tables is not None
    in_specs = [pl.BlockSpec((ts, HEAD_DIM), lambda i, h: (i, first_head + h))]
    args = [proj]
    if rope:
        in_specs += [pl.BlockSpec((ts, HEAD_DIM), lambda i, h: (i, 0)),
                     pl.BlockSpec((ts, HEAD_DIM), lambda i, h: (i, 0)),
                     pl.BlockSpec((1, HEAD_DIM), lambda i, h: (0, 0))]
        args += [tables[0], tables[1], gain.reshape(1, HEAD_DIM)]
    if transpose:
        out_spec = pl.BlockSpec((1, HEAD_DIM, ts), lambda i, h: (h, 0, i))
        out_shape = jax.ShapeDtypeStruct((n_heads, HEAD_DIM, s), jnp.bfloat16)
    else:
        out_spec = pl.BlockSpec((ts, HEAD_DIM), lambda i, h: (i, h))
        out_shape = jax.ShapeDtypeStruct((s, n_heads * HEAD_DIM), jnp.bfloat16)
    return pl.pallas_call(
        functools.partial(_head_prep_kernel, rope, transpose, scale),
        grid=(s // ts, n_heads),
        in_specs=in_specs,
        out_specs=out_spec,
        out_shape=out_shape,
        compiler_params=_cparams(("arbitrary", "arbitrary")),
        name="head_prep",
    )(*args)


def _gqa_attn_kernel(qt_ref, k_ref, vt_ref, o_ref, m_sc, l_sc, acc_sc):
    ki = pl.program_id(2)
    last = pl.num_programs(2) - 1

    @pl.when(ki == 0)
    def _():
        m_sc[...] = jnp.full(m_sc.shape, NEG_BIG, jnp.float32)
        l_sc[...] = jnp.zeros(l_sc.shape, jnp.float32)
        acc_sc[...] = jnp.zeros(acc_sc.shape, jnp.float32)

    tk = k_ref.shape[0]
    kc = min(STAGE_KEYS, tk)
    stages = [(g, c) for g in range(GQA_GROUP) for c in range(tk // kc)]

    def scores(stage):
        g, c = stage
        return jnp.dot(k_ref[c * kc:(c + 1) * kc, :], qt_ref[g], preferred_element_type=jnp.float32)

    pending = [scores(stage) for stage in stages[:SCORE_LOOKAHEAD]]
    for n, (g, c) in enumerate(stages):
        if n + SCORE_LOOKAHEAD < len(stages):
            pending.append(scores(stages[n + SCORE_LOOKAHEAD]))
        st = pending.pop(0)
        m_prev = m_sc[g]
        m_new = jnp.maximum(m_prev, jnp.max(st, axis=0, keepdims=True))
        alpha = jnp.exp2(m_prev - m_new)
        p = jnp.exp2(st - m_new)
        l_sc[g] = alpha * l_sc[g] + jnp.sum(p, axis=0, keepdims=True)
        vt = vt_ref[0, :, c * kc:(c + 1) * kc]
        acc_sc[g] = alpha * acc_sc[g] + jnp.dot(vt, p.astype(vt.dtype), preferred_element_type=jnp.float32)
        m_sc[g] = m_new

    @pl.when(ki == last)
    def _():
        for g in range(GQA_GROUP):
            o_ref[:, g * HEAD_DIM:(g + 1) * HEAD_DIM] = (acc_sc[g] / l_sc[g]).T.astype(o_ref.dtype)


def gqa_attention(qt, k, vt):
    n_q, _, s = qt.shape
    n_kv = vt.shape[0]
    tq, tk = min(GQA_TQ, s), min(GQA_TK, s)
    gw = GQA_GROUP * HEAD_DIM
    return pl.pallas_call(
        _gqa_attn_kernel,
        grid=(n_kv, s // tq, s // tk),
        in_specs=[pl.BlockSpec((GQA_GROUP, HEAD_DIM, tq), lambda h, i, j: (h, 0, i)),
                  pl.BlockSpec((tk, HEAD_DIM), lambda h, i, j: (j, h)),
                  pl.BlockSpec((1, HEAD_DIM, tk), lambda h, i, j: (h, 0, j))],
        out_specs=pl.BlockSpec((tq, gw), lambda h, i, j: (i, h)),
        out_shape=jax.ShapeDtypeStruct((s, n_q * HEAD_DIM), jnp.bfloat16),
        scratch_shapes=[pltpu.VMEM((GQA_GROUP, 1, tq), jnp.float32),
                        pltpu.VMEM((GQA_GROUP, 1, tq), jnp.float32),
                        pltpu.VMEM((GQA_GROUP, HEAD_DIM, tq), jnp.float32)],
        compiler_params=_cparams(("arbitrary", "arbitrary", "arbitrary")),
        name="gqa_attention",
    )(qt, k, vt)


def _t5_bucket(rel):
    nb = NUM_BUCKETS // 2
    max_exact = nb // 2
    n = jnp.abs(rel)
    large = max_exact + (jnp.log(jnp.maximum(n, 1).astype(jnp.float32) / max_exact)
                         / math.log(MAX_DISTANCE / max_exact) * (nb - max_exact)).astype(jnp.int32)
    large = jnp.minimum(large, nb - 1)
    return jnp.where(rel > 0, nb, 0) + jnp.where(n < max_exact, n, large)


def _bucket_saturation_distance():
    nb = NUM_BUCKETS // 2
    max_exact = nb // 2
    n = np.arange(1, 4 * MAX_DISTANCE, dtype=np.float64)
    large = max_exact + np.floor(np.log(n / max_exact) / math.log(MAX_DISTANCE / max_exact) * (nb - max_exact))
    unsat = np.nonzero(np.minimum(large, nb - 1) < nb - 1)[0]
    return int(n[unsat[-1]]) + 2


def _bias_tile_kernel(t, rows, rb_ref, o_ref):
    h = pl.program_id(0)
    d = pl.program_id(1)
    offset = (d - 1) * t

    nb = NUM_BUCKETS // 2
    win = min(rows + 2 * LANES, t)

    def body(c, carry):
        r0 = pl.multiple_of(c * rows, rows)
        w0 = pl.multiple_of(jnp.clip(r0 + offset - LANES, 0, t - win), LANES)
        if win < t:
            query_all = lax.broadcasted_iota(jnp.int32, (rows, t), 1)
            o_ref[0, 0, pl.ds(r0, rows), :] = jnp.where(query_all < w0, rb_ref[NUM_BUCKETS - 1, h] * LOG2E,
                                                        rb_ref[nb - 1, h] * LOG2E)
        key = lax.broadcasted_iota(jnp.int32, (rows, win), 0) + r0
        query = lax.broadcasted_iota(jnp.int32, (rows, win), 1) + w0
        bucket = _t5_bucket(key - query + offset)
        bias = jnp.zeros((rows, win), jnp.float32)
        for b in range(NUM_BUCKETS):
            bias = jnp.where(bucket == b, rb_ref[b, h] * LOG2E, bias)
        o_ref[0, 0, pl.ds(r0, rows), pl.ds(w0, win)] = bias
        return carry

    lax.fori_loop(0, t // rows, body, 0)


def bias_tiles(rel_bias, t):
    n_heads = rel_bias.shape[1]
    rows = min(LANES, t)
    assert _bucket_saturation_distance() <= LANES
    return pl.pallas_call(
        functools.partial(_bias_tile_kernel, t, rows),
        grid=(n_heads, 3),
        in_specs=[pl.BlockSpec(memory_space=pltpu.SMEM)],
        out_specs=pl.BlockSpec((1, 1, t, t), lambda h, d: (h, d, 0, 0)),
        out_shape=jax.ShapeDtypeStruct((n_heads, 3, t, t), jnp.float32),
        compiler_params=_cparams(("arbitrary", "arbitrary")),
        name="t5_bias_tiles",
    )(rel_bias)


def _diff_attn_kernel(lambda_init, qt_ref, k_ref, vt_ref, b_ref, rb_ref, lam_ref, sg_ref, o_ref,
                      m_sc, l_sc, acc_sc):
    hg = pl.program_id(0)
    qi = pl.program_id(1)
    ki = pl.program_id(2)
    last = pl.num_programs(2) - 1
    nb = NUM_BUCKETS // 2

    @pl.when(ki == 0)
    def _():
        m_sc[...] = jnp.full(m_sc.shape, NEG_BIG, jnp.float32)
        l_sc[...] = jnp.zeros(l_sc.shape, jnp.float32)
        acc_sc[...] = jnp.zeros(acc_sc.shape, jnp.float32)

    n_local = qt_ref.shape[0]
    w = DIFF_V_DIM

    def update(bias_tile):
        tk = k_ref.shape[0]
        kc = min(STAGE_KEYS, tk)
        stages = [(hh, mp, c) for hh in range(n_local) for mp in range(2) for c in range(tk // kc)]

        def scores(stage):
            hh, mp, c = stage
            rows = slice(mp * DIFF_HEAD_DIM, (mp + 1) * DIFF_HEAD_DIM)
            cols = slice(hh * w + mp * DIFF_HEAD_DIM, hh * w + (mp + 1) * DIFF_HEAD_DIM)
            return jnp.dot(k_ref[c * kc:(c + 1) * kc, cols], qt_ref[hh, rows, :],
                           preferred_element_type=jnp.float32)

        pending = [scores(stage) for stage in stages[:SCORE_LOOKAHEAD]]
        for n, (hh, mp, c) in enumerate(stages):
            if n + SCORE_LOOKAHEAD < len(stages):
                pending.append(scores(stages[n + SCORE_LOOKAHEAD]))
            st = pending.pop(0)
            a = hh * 2 + mp
            m_prev = m_sc[a]
            if bias_tile:
                st = st + b_ref[hh, 0, c * kc:(c + 1) * kc, :]
                m_new = jnp.maximum(m_prev, jnp.max(st, axis=0, keepdims=True))
                shift = m_new
            else:
                head = hg * n_local + hh
                bias_const = jnp.where(ki < qi, rb_ref[nb - 1, head], rb_ref[NUM_BUCKETS - 1, head]) * LOG2E
                m_new = jnp.maximum(m_prev, jnp.max(st, axis=0, keepdims=True) + bias_const)
                shift = m_new - bias_const
            alpha = jnp.exp2(m_prev - m_new)
            p = jnp.exp2(st - shift)
            l_sc[a] = alpha * l_sc[a] + jnp.sum(p, axis=0, keepdims=True)
            vt = vt_ref[hh, :, c * kc:(c + 1) * kc]
            acc_sc[a] = alpha * acc_sc[a] + jnp.dot(vt, p.astype(vt.dtype), preferred_element_type=jnp.float32)
            m_sc[a] = m_new

    near = jnp.abs(ki - qi) <= 1

    @pl.when(near)
    def _():
        update(True)

    @pl.when(jnp.logical_not(near))
    def _():
        update(False)

    @pl.when(ki == last)
    def _():
        lv = lam_ref[...]
        lam = (jnp.exp(jnp.sum(lv[0:1] * lv[1:2], axis=1, keepdims=True))
               - jnp.exp(jnp.sum(lv[2:3] * lv[3:4], axis=1, keepdims=True)) + lambda_init)
        for hh in range(n_local):
            a = 2 * hh
            o = (acc_sc[a] / l_sc[a] - lam * (acc_sc[a + 1] / l_sc[a + 1])).T
            y = o * lax.rsqrt(jnp.mean(o * o, axis=-1, keepdims=True) + NORM_EPS)
            o_ref[:, hh * w:(hh + 1) * w] = ((y * sg_ref[...]) * (1.0 - lambda_init)).astype(o_ref.dtype)


def diff_attention(qt, proj, vt, btiles, rel_bias, lam_vecs, sub_gain, lambda_init):
    n_heads, w, s = qt.shape
    t = btiles.shape[-1]
    assert t + 1 >= _bucket_saturation_distance()
    hp = min(DIFF_HEADS_PER_STEP, n_heads)
    n_groups = n_heads // hp
    return pl.pallas_call(
        functools.partial(_diff_attn_kernel, lambda_init),
        grid=(n_groups, s // t, s // t),
        in_specs=[pl.BlockSpec((hp, w, t), lambda h, i, j: (h, 0, i)),
                  pl.BlockSpec((t, hp * w), lambda h, i, j: (j, n_groups + h)),
                  pl.BlockSpec((hp, w, t), lambda h, i, j: (h, 0, j)),
                  pl.BlockSpec((hp, 1, t, t), lambda h, i, j: (h, jnp.clip(j - i + 1, 0, 2), 0, 0)),
                  pl.BlockSpec(memory_space=pltpu.SMEM),
                  pl.BlockSpec((4, DIFF_HEAD_DIM), lambda h, i, j: (0, 0)),
                  pl.BlockSpec((1, w), lambda h, i, j: (0, 0))],
        out_specs=pl.BlockSpec((t, hp * w), lambda h, i, j: (i, h)),
        out_shape=jax.ShapeDtypeStruct((s, n_heads * w), jnp.bfloat16),
        scratch_shapes=[pltpu.VMEM((2 * hp, 1, t), jnp.float32),
                        pltpu.VMEM((2 * hp, 1, t), jnp.float32),
                        pltpu.VMEM((2 * hp, w, t), jnp.float32)],
        compiler_params=_cparams(("arbitrary", "arbitrary", "arbitrary")),
        name="diff_attention",
    )(qt, proj, vt, btiles, rel_bias, lam_vecs, sub_gain.reshape(1, w))


def _bf16_bits(x_bf16):
    return pltpu.bitcast(x_bf16.astype(jnp.float32), jnp.uint32)


def _route_kernel(x_ref, g_ref, rw_ref, h_ref, aff_ref):
    x = x_ref[...]
    y = x * lax.rsqrt(jnp.mean(x * x, axis=-1, keepdims=True) + NORM_EPS)
    hb = (y * g_ref[...]).astype(jnp.bfloat16)
    half = hb.shape[1] // 2
    h_ref[...] = (_bf16_bits(hb[:, :half]) >> 16) | _bf16_bits(hb[:, half:])
    logits = lax.dot_general(rw_ref[...], hb, (((1,), (1,)), ((), ())),
                             preferred_element_type=jnp.float32)
    e = jnp.exp(logits - jnp.max(logits, axis=0, keepdims=True))
    aff_ref[...] = e / jnp.sum(e, axis=0, keepdims=True)


def route(x, g, router_w_t):
    s, d = x.shape
    n_e = router_w_t.shape[0]
    ts = min(ROUTE_ROWS, s)
    return pl.pallas_call(
        _route_kernel,
        grid=(s // ts,),
        in_specs=[pl.BlockSpec((ts, d), lambda i: (i, 0)),
                  pl.BlockSpec((1, d), lambda i: (0, 0)),
                  pl.BlockSpec((n_e, d), lambda i: (0, 0))],
        out_specs=[pl.BlockSpec((ts, d // 2), lambda i: (i, 0)),
                   pl.BlockSpec((n_e, ts), lambda i: (0, i))],
        out_shape=[jax.ShapeDtypeStruct((s, d // 2), jnp.uint32),
                   jax.ShapeDtypeStruct((n_e, s), jnp.float32)],
        compiler_params=_cparams(("arbitrary",)),
        name="moe_route",
    )(x, g.reshape(1, d), router_w_t)


def _split3_bf16(x):
    hi = x.astype(jnp.bfloat16)
    r1 = x - hi.astype(jnp.float32)
    mid = r1.astype(jnp.bfloat16)
    lo = (r1 - mid.astype(jnp.float32)).astype(jnp.bfloat16)
    return hi, mid, lo


def _select_kernel(cap, aff_ref, idx_ref, gate_ref):
    a = aff_ref[0]
    nch = a.shape[0]
    bits = pltpu.bitcast(a, jnp.int32)

    def count(mask):
        c = jnp.sum(mask.astype(jnp.float32), axis=1, keepdims=True)
        return jnp.sum(c, axis=0, keepdims=True)

    thr = jnp.zeros((1, 1), jnp.int32)
    for b in range(30, -1, -1):
        cand = thr | (1 << b)
        thr = jnp.where(count(bits >= cand) >= cap, cand, thr)

    li = lax.broadcasted_iota(jnp.int32, (LANES, LANES), 0)
    lj = lax.broadcasted_iota(jnp.int32, (LANES, LANES), 1)
    tri_incl = (li <= lj).astype(jnp.bfloat16)
    ci = lax.broadcasted_iota(jnp.int32, (nch, nch), 0)
    cj = lax.broadcasted_iota(jnp.int32, (nch, nch), 1)
    before_rows = (cj < ci).astype(jnp.bfloat16)
    before_cols = (ci < cj).astype(jnp.bfloat16)
    ones8 = jnp.ones((8, LANES), jnp.bfloat16)

    def prefix(mask_f32):
        mb = mask_f32.astype(jnp.bfloat16)
        incl = jnp.dot(mb, tri_incl, preferred_element_type=jnp.float32)
        start_col = jnp.sum(jnp.dot(before_rows, mb, preferred_element_type=jnp.float32),
                            axis=1, keepdims=True)
        tot = lax.dot_general(ones8, mb, (((1,), (1,)), ((), ())), preferred_element_type=jnp.float32)
        start_lane = jnp.dot(tot.astype(jnp.bfloat16), before_cols,
                             preferred_element_type=jnp.float32)[0:1]
        return incl, start_col, start_lane

    gt = bits > thr
    eq = bits == thr
    need = cap - count(gt)
    eq_f = eq.astype(jnp.float32)
    incl_eq, start_eq, _ = prefix(eq_f)
    keep = jnp.logical_and(eq, (incl_eq - eq_f + start_eq) < need)
    sel_f = jnp.logical_or(gt, keep).astype(jnp.float32)
    incl, _, start_lane = prefix(sel_f)

    slot = lax.broadcasted_iota(jnp.int32, (cap, 1), 0).astype(jnp.float32)
    chunk = jnp.sum((start_lane <= slot).astype(jnp.float32), axis=1, keepdims=True) - 1.0
    chunk_iota = lax.broadcasted_iota(jnp.int32, (cap, nch), 1).astype(jnp.float32)
    onehot = chunk_iota == chunk
    chunk_start = jnp.sum(jnp.where(onehot, start_lane, 0.0), axis=1, keepdims=True)
    onehot_b = onehot.astype(jnp.bfloat16)
    rank = slot - chunk_start
    incl_rows = jnp.dot(onehot_b, incl.astype(jnp.bfloat16), preferred_element_type=jnp.float32)
    lane = jnp.sum((incl_rows <= rank).astype(jnp.float32), axis=1, keepdims=True)
    idx_ref[0] = (chunk * LANES + lane).astype(jnp.int32)

    hi, mid, lo = _split3_bf16(a)
    arow = (jnp.dot(onehot_b, hi, preferred_element_type=jnp.float32)
            + jnp.dot(onehot_b, mid, preferred_element_type=jnp.float32)
            + jnp.dot(onehot_b, lo, preferred_element_type=jnp.float32))
    lane_iota = lax.broadcasted_iota(jnp.int32, (cap, LANES), 1).astype(jnp.float32)
    gate_ref[0] = jnp.sum(jnp.where(lane_iota == lane, arow, 0.0), axis=1, keepdims=True)


def select(aff_t, cap):
    n_e, s = aff_t.shape
    nch = s // LANES
    return pl.pallas_call(
        functools.partial(_select_kernel, cap),
        grid=(n_e,),
        in_specs=[pl.BlockSpec((1, nch, LANES), lambda e: (e, 0, 0))],
        out_specs=[pl.BlockSpec((1, cap, 1), lambda e: (e, 0, 0)),
                   pl.BlockSpec((1, cap, 1), lambda e: (e, 0, 0))],
        out_shape=[jax.ShapeDtypeStruct((n_e, cap, 1), jnp.int32),
                   jax.ShapeDtypeStruct((n_e, cap, 1), jnp.float32)],
        compiler_params=_cparams(("arbitrary",)),
        name="moe_select",
    )(aff_t.reshape(n_e, nch, LANES))


def _row_copy_loop(hbm, vmem, sem, idx_ref, base, n_rows, to_vmem, wait):
    def body(r, c):
        tok = idx_ref[base + r]
        hbm_row = hbm.at[pl.ds(tok, 1)]
        vmem_row = vmem.at[pl.ds(r, 1)]
        copy = (pltpu.make_async_copy(hbm_row, vmem_row, sem) if to_vmem
                else pltpu.make_async_copy(vmem_row, hbm_row, sem))
        if wait:
            copy.wait()
        else:
            copy.start()
        return c

    lax.fori_loop(0, n_rows, body, 0, unroll=ROW_DMA_UNROLL)


def _row_copies(hbm, vmem, sem, idx_ref, base, n_rows, to_vmem):
    _row_copy_loop(hbm, vmem, sem, idx_ref, base, n_rows, to_vmem, wait=False)
    _row_copy_loop(hbm, vmem, sem, idx_ref, base, n_rows, to_vmem, wait=True)


def _expert_kernel(n_k, blocks_per_e, layer, idx_ref, h_hbm, gate_ref, wg_hbm, wu_hbm, wd_hbm, y_ref,
                   xs, kbuf, nbuf, g_acc, u_acc, act, sems, ksems, nsems):
    blk = pl.program_id(0)
    j = pl.program_id(1)
    n_blk = pl.num_programs(0)
    rows = xs.shape[1]
    kc = kbuf.shape[2]
    slot = blk % 2
    n_units = n_blk * n_k

    def gather(b, s, wait):
        _row_copy_loop(h_hbm, xs.at[s], sems.at[s], idx_ref, b * rows, rows, True, wait)

    def gate_up_copies(u):
        e = (u // n_k) // blocks_per_e
        r0 = pl.multiple_of((u % n_k) * kc, kc)
        s = u % WEIGHT_BUFFERS
        return (pltpu.make_async_copy(wg_hbm.at[layer, e, pl.ds(r0, kc), :], kbuf.at[s, 0], ksems.at[s]),
                pltpu.make_async_copy(wu_hbm.at[layer, e, pl.ds(r0, kc), :], kbuf.at[s, 1], ksems.at[s]))

    def down_copy(u):
        e = (u // n_k) // blocks_per_e
        c0 = pl.multiple_of((u % n_k) * kc, kc)
        s = u % WEIGHT_BUFFERS
        return pltpu.make_async_copy(wd_hbm.at[layer, e, :, pl.ds(c0, kc)], nbuf.at[s], nsems.at[s])

    def start_gate_up(u):
        @pl.when(u < n_units)
        def _():
            for copy in gate_up_copies(u):
                copy.start()

    def start_down(u):
        @pl.when(u < n_units)
        def _():
            down_copy(u).start()

    @pl.when(jnp.logical_and(blk == 0, j == 0))
    def _():
        for u in range(WEIGHT_BUFFERS - 1):
            start_gate_up(jnp.int32(u))
            start_down(jnp.int32(u))

    @pl.when(j == 0)
    def _():
        @pl.when(blk == 0)
        def _():
            gather(0, 0, wait=False)

        gather(blk, slot, wait=True)

        @pl.when(blk + 1 < n_blk)
        def _():
            gather(blk + 1, 1 - slot, wait=False)

        g_acc[...] = jnp.zeros(g_acc.shape, jnp.float32)
        u_acc[...] = jnp.zeros(u_acc.shape, jnp.float32)

    half = n_k // 2
    for jj in range(n_k):
        @pl.when(j == jj)
        def _():
            unit = blk * n_k + jj
            start_gate_up(unit + (WEIGHT_BUFFERS - 1))
            for copy in gate_up_copies(unit):
                copy.wait()
            ws = unit % WEIGHT_BUFFERS
            words = xs[slot, :, (jj % half) * kc:(jj % half + 1) * kc]
            bits = (words << 16) if jj < half else (words & jnp.uint32(0xFFFF0000))
            xk = pltpu.bitcast(bits, jnp.float32).astype(jnp.bfloat16)
            g_acc[...] += jnp.dot(xk, kbuf[ws, 0].astype(jnp.bfloat16), preferred_element_type=jnp.float32)
            u_acc[...] += jnp.dot(xk, kbuf[ws, 1].astype(jnp.bfloat16), preferred_element_type=jnp.float32)

    @pl.when(j == n_k - 1)
    def _():
        g = g_acc[...]
        act[...] = ((g * (1.0 / (1.0 + jnp.exp(-g)))) * u_acc[...]).astype(act.dtype)

    @pl.when(j >= n_k)
    def _():
        unit = blk * n_k + (j - n_k)
        start_down(unit + (WEIGHT_BUFFERS - 1))
        down_copy(unit).wait()
        wd = nbuf[unit % WEIGHT_BUFFERS].astype(jnp.bfloat16)
        y = jnp.dot(act[...], wd, preferred_element_type=jnp.float32)
        y_ref[...] = y * gate_ref[...]


def expert_ffn(h_packed, idx_flat, gate_col, w_gate, w_up, w_down, layer):
    d = 2 * h_packed.shape[1]
    _, n_e, _, ff = w_gate.shape
    n_slots = idx_flat.shape[0]
    cap = n_slots // n_e
    rows = min(FFN_ROWS, cap)
    blocks_per_e = cap // rows
    kc = min(FFN_KCHUNK, d // 2)
    n_k = d // kc
    n_n = d // kc
    grid_spec = pltpu.PrefetchScalarGridSpec(
        num_scalar_prefetch=1,
        grid=(n_slots // rows, n_k + n_n),
        in_specs=[pl.BlockSpec(memory_space=pl.ANY),
                  pl.BlockSpec((rows, 1), lambda b, j, idx: (b, 0)),
                  pl.BlockSpec(memory_space=pl.ANY),
                  pl.BlockSpec(memory_space=pl.ANY),
                  pl.BlockSpec(memory_space=pl.ANY)],
        out_specs=pl.BlockSpec((rows, kc), lambda b, j, idx: (b, jnp.maximum(j - n_k, 0))),
        scratch_shapes=[pltpu.VMEM((2, rows, d // 2), jnp.uint32),
                        pltpu.VMEM((WEIGHT_BUFFERS, 2, kc, ff), jnp.float32),
                        pltpu.VMEM((WEIGHT_BUFFERS, ff, kc), jnp.float32),
                        pltpu.VMEM((rows, ff), jnp.float32),
                        pltpu.VMEM((rows, ff), jnp.float32),
                        pltpu.VMEM((rows, ff), jnp.bfloat16),
                        pltpu.SemaphoreType.DMA((2,)),
                        pltpu.SemaphoreType.DMA((WEIGHT_BUFFERS,)),
                        pltpu.SemaphoreType.DMA((WEIGHT_BUFFERS,))],
    )
    return pl.pallas_call(
        functools.partial(_expert_kernel, n_k, blocks_per_e, layer),
        grid_spec=grid_spec,
        out_shape=jax.ShapeDtypeStruct((n_slots, d), jnp.float32),
        compiler_params=_cparams(("arbitrary", "arbitrary")),
        name="moe_expert_ffn",
    )(idx_flat, h_packed, gate_col, w_gate, w_up, w_down)


def _combine_kernel(idx_ref, x_in_hbm, y_ref, x_hbm, stage, sem_in, sem_out):
    del x_in_hbm
    blk = pl.program_id(0)
    rows = stage.shape[0]
    _row_copies(x_hbm, stage, sem_in, idx_ref, blk * rows, rows, True)
    stage[...] += y_ref[...]
    _row_copies(x_hbm, stage, sem_out, idx_ref, blk * rows, rows, False)


def combine(x, y_slots, idx_flat, n_e):
    s, d = x.shape
    n_slots = idx_flat.shape[0]
    cap = n_slots // n_e
    rows = min(COMBINE_ROWS, cap)
    grid_spec = pltpu.PrefetchScalarGridSpec(
        num_scalar_prefetch=1,
        grid=(n_slots // rows,),
        in_specs=[pl.BlockSpec(memory_space=pl.ANY),
                  pl.BlockSpec((rows, d), lambda b, idx: (b, 0))],
        out_specs=pl.BlockSpec(memory_space=pl.ANY),
        scratch_shapes=[pltpu.VMEM((rows, d), jnp.float32),
                        pltpu.SemaphoreType.DMA(()),
                        pltpu.SemaphoreType.DMA(())],
    )
    return pl.pallas_call(
        _combine_kernel,
        grid_spec=grid_spec,
        out_shape=jax.ShapeDtypeStruct((s, d), jnp.float32),
        input_output_aliases={1: 0},
        compiler_params=_cparams(("arbitrary",)),
        name="moe_combine",
    )(idx_flat, x, y_slots)


def expert_choice_ffn(x, norm_g, router_w, w_gate, w_up, w_down, layer):
    s, d = x.shape
    n_e = w_gate.shape[1]
    cap = EC_CAPACITY_FACTOR * s // n_e
    h_packed, aff_t = route(x, norm_g, router_w.T.astype(jnp.bfloat16))
    idx, gate = select(aff_t, cap)
    idx_flat = idx.reshape(n_e * cap)
    y = expert_ffn(h_packed, idx_flat, gate.reshape(n_e * cap, 1), w_gate, w_up, w_down, layer)
    return combine(x, y, idx_flat, n_e)


def kernel(x, attn_norm, ffn_norm, final_norm, gqa_w_in, gqa_q_norm, gqa_k_norm, gqa_w_out,
           diff_w_in, diff_lambda_q1, diff_lambda_k1, diff_lambda_q2, diff_lambda_k2,
           diff_sub_norm, diff_w_out, rel_bias, router_w, expert_w_gate, expert_w_up,
           expert_w_down):
    b, s, d = x.shape
    assert b == 1
    bf = jnp.bfloat16
    depth = attn_norm.shape[0]
    n_q = d // HEAD_DIM
    n_kv = n_q // GQA_GROUP
    n_diff = d // DIFF_V_DIM
    diff_t = min(DIFF_T, s)
    btiles = bias_tiles(rel_bias, diff_t) if depth > 1 else None

    xs = x.reshape(s, d)
    rope_tables = _rope_tables(s)
    for i in range(depth):
        j = i // N_MIXERS
        if i % N_MIXERS == 0:
            h = rmsnorm(xs, attn_norm[i], bf)
            proj = matmul(h, gqa_w_in, j, jnp.float32)
            qt = head_prep(proj, 0, n_q, rope_tables, gqa_q_norm[j], scale=HEAD_DIM ** -0.5 * LOG2E, transpose=True)
            k = head_prep(proj, n_q, n_kv, rope_tables, gqa_k_norm[j])
            vt = head_prep(proj, n_q + n_kv, n_kv, transpose=True)
            o = gqa_attention(qt, k, vt)
            xs = matmul(o, gqa_w_out, j, jnp.float32, residual=xs)
        else:
            lambda_init = 0.8 - 0.6 * math.exp(-0.3 * i)
            dqk = n_diff * 2 * DIFF_HEAD_DIM
            h = rmsnorm(xs, attn_norm[i], bf)
            proj = matmul(h, diff_w_in, j, bf, col_scale=(dqk, DIFF_HEAD_DIM ** -0.5 * LOG2E))
            qt = head_prep(proj, 0, 2 * n_diff, transpose=True).reshape(n_diff, DIFF_V_DIM, s)
            vt = head_prep(proj, 4 * n_diff, 2 * n_diff, transpose=True).reshape(n_diff, DIFF_V_DIM, s)
            lam_vecs = jnp.stack([diff_lambda_q1[j], diff_lambda_k1[j], diff_lambda_q2[j], diff_lambda_k2[j]])
            o = diff_attention(qt, proj, vt, btiles, rel_bias, lam_vecs, diff_sub_norm[j], lambda_init)
            xs = matmul(o, diff_w_out, j, jnp.float32, residual=xs)
        xs = expert_choice_ffn(xs, ffn_norm[i], router_w[i], expert_w_gate, expert_w_up, expert_w_down, i)
    return rmsnorm(xs, final_norm, jnp.float32).reshape(b, s, d)
```
